```python
import math
import jax
import jax.numpy as jnp
from jax import lax
import numpy as np

D_MODEL = 2048
BATCH = 2
SEQ = 4096
DEPTH = 2

CHUNK = 64
N_META = 16
CONV_WIDTH = D_MODEL // 2
CONV_GROUPS = 8
CONV_KERNEL = 31
HEAD_DIM = 128
N_DELTA_HEADS = (D_MODEL // 2) // HEAD_DIM
DELTA_WIDTH = N_DELTA_HEADS * HEAD_DIM
SHORT_CONV = 4
MIX_WIDTH = CONV_WIDTH + DELTA_WIDTH
N_GROUPS = 4
EXPERTS_PER_GROUP = 8
N_EXPERTS = N_GROUPS * EXPERTS_PER_GROUP
TOP_K = 2
D_EXPERT = 512
EPS = 1e-6
IN_COLS = 2 * CONV_WIDTH + 4 * DELTA_WIDTH + 2 * N_DELTA_HEADS

kernel_name = 'hybrid_conformer_deltanet_hmoe'


def rmsnorm(x, g):
    xf = x.astype(jnp.float32)
    y = xf * lax.rsqrt(jnp.mean(xf * xf, axis=-1, keepdims=True) + EPS)
    return (y * g.astype(jnp.float32)).astype(x.dtype)


def causal_depthwise_conv(x, w):
    k = w.shape[0]
    xp = jnp.pad(x, ((0, 0), (k - 1, 0), (0, 0)))
    return lax.conv_general_dilated(
        xp, w[:, None, :].astype(x.dtype), window_strides=(1,), padding='VALID',
        dimension_numbers=('NWC', 'WIO', 'NWC'), feature_group_count=x.shape[-1])


def split_in_proj(p):
    c, dl, h = CONV_WIDTH, DELTA_WIDTH, N_DELTA_HEADS
    idx = [c, 2 * c, 2 * c + 3 * dl, 2 * c + 4 * dl, 2 * c + 4 * dl + h]
    return jnp.split(p, idx, axis=-1)


def conformer_conv(glu_a, glu_b, w_dw, b_dw, ln_g, ln_b):
    y = glu_a * jax.nn.sigmoid(glu_b)
    y = causal_depthwise_conv(y, w_dw) + b_dw.astype(y.dtype)
    b_, l_, _ = y.shape
    yg = y.reshape(b_, l_, CONV_GROUPS, CONV_WIDTH // CONV_GROUPS).astype(jnp.float32)
    mu = jnp.mean(yg, axis=-1, keepdims=True)
    var = jnp.mean(jnp.square(yg - mu), axis=-1, keepdims=True)
    yg = (yg - mu) * lax.rsqrt(var + EPS)
    y = yg.reshape(b_, l_, CONV_WIDTH) * ln_g.astype(jnp.float32) + ln_b.astype(jnp.float32)
    return jax.nn.silu(y).astype(glu_a.dtype)


def chunk_gated_delta_rule(q, k, v, beta, g):
    b_, l_, h_, d_ = q.shape
    pad = (-l_) % CHUNK

    def prep(t):
        t = jnp.pad(t, ((0, 0), (pad, 0)) + ((0, 0),) * (t.ndim - 2))
        t = t.reshape((b_, -1, CHUNK) + t.shape[2:])
        return jnp.moveaxis(t, 3, 2)

    q, k, v, beta, g = prep(q), prep(k), prep(v), prep(beta), prep(g)
    gc = jnp.cumsum(g, axis=-1)
    incl = jnp.tril(jnp.ones((CHUNK, CHUNK), dtype=bool))
    strict = jnp.tril(jnp.ones((CHUNK, CHUNK), dtype=bool), -1)
    decay = jnp.exp(jnp.where(incl, gc[..., :, None] - gc[..., None, :], -jnp.inf))
    kb = k * beta[..., None]
    vb = v * beta[..., None]
    m = jnp.einsum('bnhid,bnhjd->bnhij', kb, k) * decay
    a_mat = jnp.where(strict, m, 0.0) + jnp.eye(CHUNK, dtype=jnp.float32)
    rhs = jnp.concatenate([vb, kb * jnp.exp(gc)[..., None]], axis=-1)
    sol = lax.linalg.triangular_solve(a_mat, rhs, left_side=True, lower=True, unit_diagonal=True)
    u, w = sol[..., :d_], sol[..., d_:]
    intra = jnp.einsum('bnhid,bnhjd->bnhij', q, k) * decay
    q_dec = q * jnp.exp(gc)[..., None]
    k_dec = k * jnp.exp(gc[..., -1:] - gc)[..., None]
    g_last = jnp.exp(gc[..., -1])
    xs = tuple(jnp.moveaxis(t, 1, 0) for t in (u, w, q_dec, k_dec, intra, g_last))

    def step(s, inp):
        u_c, w_c, qd_c, kd_c, intra_c, gl_c = inp
        v_new = u_c - jnp.einsum('bhcd,bhde->bhce', w_c, s)
        o_c = jnp.einsum('bhcd,bhde->bhce', qd_c, s) + jnp.einsum('bhij,bhje->bhie', intra_c, v_new)
        s = s * gl_c[..., None, None] + jnp.einsum('bhcd,bhce->bhde', kd_c, v_new)
        return s, o_c

    s0 = jnp.zeros((b_, h_, d_, d_), jnp.float32)
    _, o = lax.scan(step, s0, xs)
    o = jnp.transpose(o, (1, 0, 3, 2, 4)).reshape(b_, -1, h_, d_)
    return o[:, pad:]


def gated_deltanet(qkv, z, beta_logit, decay_logit, sc_w, a_log, dt_bias, norm_g):
    b_, l_, _ = qkv.shape
    qkv = jax.nn.silu(causal_depthwise_conv(qkv, sc_w))
    q, k, v = (t.reshape(b_, l_, N_DELTA_HEADS, HEAD_DIM).astype(jnp.float32)
               for t in jnp.split(qkv, 3, axis=-1))
    q = q * lax.rsqrt(jnp.sum(q * q, axis=-1, keepdims=True) + EPS) * (HEAD_DIM ** -0.5)
    k = k * lax.rsqrt(jnp.sum(k * k, axis=-1, keepdims=True) + EPS)
    beta = jax.nn.sigmoid(beta_logit.astype(jnp.float32))
    g = -jnp.exp(a_log.astype(jnp.float32)) * jax.nn.softplus(
        decay_logit.astype(jnp.float32) + dt_bias.astype(jnp.float32))
    o = chunk_gated_delta_rule(q, k, v, beta, g)
    o = o * lax.rsqrt(jnp.mean(o * o, axis=-1, keepdims=True) + EPS) * norm_g.astype(jnp.float32)
    o = o * jax.nn.silu(z.reshape(b_, l_, N_DELTA_HEADS, HEAD_DIM).astype(jnp.float32))
    return o.reshape(b_, l_, DELTA_WIDTH).astype(z.dtype)


def hier_moe(u, w_group, b_group, w_router, b_router, w_gate, w_up, w_down):
    b_, l_, d_ = u.shape
    t = u.reshape(-1, d_)
    glog = (t @ w_group).astype(jnp.float32) + b_group.astype(jnp.float32)
    gprob = jax.nn.softmax(glog, axis=-1)
    gsel = jnp.argmax(glog, axis=-1)
    p_group = jnp.take_along_axis(gprob, gsel[:, None], axis=-1)
    elog = ((t @ w_router).astype(jnp.float32) + b_router.astype(jnp.float32)).reshape(
        -1, N_GROUPS, EXPERTS_PER_GROUP)
    elog_sel = jnp.take_along_axis(elog, gsel[:, None, None], axis=1)[:, 0]
    top_v, top_i = lax.top_k(elog_sel, TOP_K)
    w_k = jax.nn.softmax(top_v, axis=-1) * p_group
    within = jnp.sum(jax.nn.one_hot(top_i, EXPERTS_PER_GROUP, dtype=jnp.float32) * w_k[..., None], axis=1)
    comb = jax.nn.one_hot(gsel, N_GROUPS, dtype=jnp.float32)[:, :, None] * within[:, None, :]
    comb = comb.reshape(-1, N_EXPERTS).astype(t.dtype)
    out = jnp.zeros_like(t)
    for gi in range(N_GROUPS):
        sl = slice(gi * EXPERTS_PER_GROUP, (gi + 1) * EXPERTS_PER_GROUP)
        hg = jnp.einsum('nd,edf->nef', t, w_gate[sl])
        hu = jnp.einsum('nd,edf->nef', t, w_up[sl])
        hid = jax.nn.silu(hg) * hu * comb[:, sl, None]
        out = out + jnp.einsum('nef,efd->nd', hid, w_down[sl])
    return out.reshape(b_, l_, d_)


def setup_inputs(seed: int = 0) -> dict:
    key = jax.random.key(seed)
    ks = jax.random.split(key, 24)
    f32 = jnp.float32

    def nrm(k, shape, scale):
        return jax.random.normal(k, shape, f32) * scale

    dt = jnp.exp(jax.random.uniform(ks[9], (DEPTH, N_DELTA_HEADS), f32, math.log(1e-3), math.log(1e-1)))
    return {
        'x': nrm(ks[0], (BATCH, SEQ, D_MODEL), 1.0),
        'meta': nrm(ks[1], (N_META, D_MODEL), 1.0),
        'attn_norm': 1.0 + nrm(ks[2], (DEPTH, D_MODEL), 0.02),
        'w_in': nrm(ks[3], (DEPTH, D_MODEL, IN_COLS), D_MODEL ** -0.5),
        'conv_dw_w': nrm(ks[4], (DEPTH, CONV_KERNEL, CONV_WIDTH), CONV_KERNEL ** -0.5),
        'conv_dw_b': nrm(ks[5], (DEPTH, CONV_WIDTH), 0.02),
        'conv_ln_g': 1.0 + nrm(ks[6], (DEPTH, CONV_WIDTH), 0.02),
        'conv_ln_b': nrm(ks[7], (DEPTH, CONV_WIDTH), 0.02),
        'short_conv_w': nrm(ks[8], (DEPTH, SHORT_CONV, 3 * DELTA_WIDTH), SHORT_CONV ** -0.5),
        'a_log': jnp.log(jax.random.uniform(ks[10], (DEPTH, N_DELTA_HEADS), f32, 1.0, 16.0)),
        'dt_bias': dt + jnp.log(-jnp.expm1(-dt)),
        'delta_norm_g': 1.0 + nrm(ks[11], (DEPTH, HEAD_DIM), 0.02),
        'w_out': nrm(ks[12], (DEPTH, MIX_WIDTH, D_MODEL), MIX_WIDTH ** -0.5),
        'ffn_norm': 1.0 + nrm(ks[13], (DEPTH, D_MODEL), 0.02),
        'w_group': nrm(ks[14], (DEPTH, D_MODEL, N_GROUPS), D_MODEL ** -0.5),
        'b_group': nrm(ks[15], (DEPTH, N_GROUPS), 0.01),
        'w_router': nrm(ks[16], (DEPTH, D_MODEL, N_EXPERTS), D_MODEL ** -0.5),
        'b_router': nrm(ks[17], (DEPTH, N_EXPERTS), 0.01),
        'w_gate': nrm(ks[18], (DEPTH, N_EXPERTS, D_MODEL, D_EXPERT), D_MODEL ** -0.5),
        'w_up': nrm(ks[19], (DEPTH, N_EXPERTS, D_MODEL, D_EXPERT), D_MODEL ** -0.5),
        'w_down': nrm(ks[20], (DEPTH, N_EXPERTS, D_EXPERT, D_MODEL), D_EXPERT ** -0.5),
        'final_norm': 1.0 + nrm(ks[21], (D_MODEL,), 0.02),
    }


def reference(x, meta, attn_norm, w_in, conv_dw_w, conv_dw_b, conv_ln_g, conv_ln_b,
              short_conv_w, a_log, dt_bias, delta_norm_g, w_out, ffn_norm, w_group,
              b_group, w_router, b_router, w_gate, w_up, w_down, final_norm):
    b_ = x.shape[0]
    meta_b = jnp.broadcast_to(meta[None].astype(x.dtype), (b_, N_META, D_MODEL))
    h = jnp.concatenate([meta_b, x], axis=1)
    for l in range(DEPTH):
        u = rmsnorm(h, attn_norm[l])
        glu_a, glu_b, qkv, z, beta_logit, decay_logit = split_in_proj(u @ w_in[l])
        y_conv = conformer_conv(glu_a, glu_b, conv_dw_w[l], conv_dw_b[l], conv_ln_g[l], conv_ln_b[l])
        y_delta = gated_deltanet(qkv, z, beta_logit, decay_logit, short_conv_w[l],
                                 a_log[l], dt_bias[l], delta_norm_g[l])
        h = h + jnp.concatenate([y_conv, y_delta], axis=-1) @ w_out[l]
        u = rmsnorm(h, ffn_norm[l])
        h = h + hier_moe(u, w_group[l], b_group[l], w_router[l], b_router[l],
                         w_gate[l], w_up[l], w_down[l])
    return rmsnorm(h, final_norm)[:, N_META:]
```

```python
import functools

import jax
import jax.numpy as jnp
from jax import lax
from jax.experimental import pallas as pl
from jax.experimental.pallas import tpu as pltpu

F32 = jnp.float32
BF16 = jnp.bfloat16
HI = lax.Precision.HIGHEST

EPS = 1e-6
CHUNK = 64
N_META = 16
CONV_GROUPS = 8
CONV_KERNEL = 31
HEAD_DIM = 128
SHORT_CONV = 4
N_GROUPS = 4
EXPERTS_PER_GROUP = 8
N_EXPERTS = N_GROUPS * EXPERTS_PER_GROUP
LANES = 128
SUBLANES = 8
CONV_HIST = 32
SC_HIST = 8
VMEM_LIMIT = 56 * 1024 * 1024

INPROJ_TM, INPROJ_TN = 640, 512
CONV_CHUNKS = 13
DELTA_CHUNKS = 5
OUTPROJ_TM, OUTPROJ_TN = 832, 512
ROUTER_TM = 640
MOE_TM = 416
FINAL_T = 64


def _divisor_tile(n, cap, mult):
    best = None
    for t in range(mult, min(n, cap) + 1, mult):
        if n % t == 0:
            best = t
    if best is None:
        raise ValueError(f"no tile for n={n} cap={cap} mult={mult}")
    return best


def _params(sem):
    return pltpu.CompilerParams(dimension_semantics=sem, vmem_limit_bytes=VMEM_LIMIT)


def _dot(a, b, precision=None):
    return jnp.dot(a, b, preferred_element_type=F32, precision=precision)


def _dot_nt(a, b, precision=None):
    return lax.dot_general(a, b, (((1,), (1,)), ((), ())), preferred_element_type=F32, precision=precision)


def _silu(x):
    return x * jax.nn.sigmoid(x)


def _softplus(x):
    return jnp.maximum(x, 0.0) + jnp.log1p(jnp.exp(-jnp.abs(x)))


def _causal_taps(win, tap_w, n_taps, first_tap):
    acc = jnp.zeros((CHUNK,) + win.shape[1:], F32)
    for res in range(SUBLANES):
        offs = [o for o in range(first_tap, first_tap + n_taps) if o % SUBLANES == res]
        if not offs:
            continue
        shifted = win[res:, :] if res else win
        for o in offs:
            a = o - res
            acc = acc + tap_w(o - first_tap) * shifted[a:a + CHUNK, :]
    return acc


def _inproj_kernel(h_ref, g_ref, w_ref, wbd_ref, wbdt_ref, p_ref, bdc_ref, bdr_ref, u_sc, *, tm, lp, pad, nb):
    i = pl.program_id(0)
    j = pl.program_id(1)

    @pl.when(j == 0)
    def _():
        x = h_ref[...]
        ms = jnp.mean(x * x, axis=-1, keepdims=True)
        u = x * lax.rsqrt(ms + EPS) * g_ref[...]
        row = i * tm + lax.broadcasted_iota(jnp.int32, (tm, 1), 0)
        valid = (row >= pad) & (row < lp)
        for b in range(1, nb):
            valid = valid | ((row >= b * lp + pad) & (row < (b + 1) * lp))
        ub = jnp.where(valid, u, 0.0).astype(BF16)
        u_sc[...] = ub
        bdc_ref[...] = _dot(ub, wbd_ref[...].astype(BF16))
        bdr_ref[...] = _dot_nt(wbdt_ref[...].astype(BF16), ub)

    p_ref[...] = _dot(u_sc[...], w_ref[...].astype(BF16))


def _inproj(h, norm_g, w_in, layer, n_main, lp, pad, nb):
    r, d = h.shape
    n_bd = w_in.shape[2] - n_main
    tm = _divisor_tile(r, INPROJ_TM, LANES) if r > LANES else r
    tn = _divisor_tile(n_main, INPROJ_TN, LANES)
    w_bd = w_in[layer, :, n_main:]
    kern = functools.partial(_inproj_kernel, tm=tm, lp=lp, pad=pad, nb=nb)
    return pl.pallas_call(
        kern,
        grid=(r // tm, n_main // tn),
        in_specs=[
            pl.BlockSpec((tm, d), lambda i, j: (i, 0)),
            pl.BlockSpec((1, d), lambda i, j: (0, 0)),
            pl.BlockSpec((None, d, tn), lambda i, j: (layer, 0, j)),
            pl.BlockSpec((d, n_bd), lambda i, j: (0, 0)),
            pl.BlockSpec((n_bd, d), lambda i, j: (0, 0)),
        ],
        out_specs=[
            pl.BlockSpec((tm, tn), lambda i, j: (i, j)),
            pl.BlockSpec((tm, n_bd), lambda i, j: (i, 0)),
            pl.BlockSpec((n_bd, tm), lambda i, j: (0, i)),
        ],
        out_shape=[
            jax.ShapeDtypeStruct((r, n_main), F32),
            jax.ShapeDtypeStruct((r, n_bd), F32),
            jax.ShapeDtypeStruct((n_bd, r), F32),
        ],
        scratch_shapes=[pltpu.VMEM((tm, d), BF16)],
        compiler_params=_params(("arbitrary", "arbitrary")),
        name="inproj",
    )(h, norm_g.reshape(1, d), w_in, w_bd, w_bd.T)


def _conv_kernel(a_ref, b_ref, w_ref, bias_ref, lg_ref, lb_ref, o_ref, ybuf, *, t_rows, width):
    t = pl.program_id(1)
    n_blk = t_rows // CHUNK

    @pl.when(t == 0)
    def _():
        ybuf[0:CONV_HIST, :] = jnp.zeros((CONV_HIST, width), F32)

    @pl.when(t > 0)
    def _():
        ybuf[0:CONV_HIST, :] = ybuf[t_rows:t_rows + CONV_HIST, :]

    def glu_body(r, c):
        r0 = pl.multiple_of(r * CHUNK, CHUNK)
        a = a_ref[pl.ds(r0, CHUNK), :]
        g = b_ref[pl.ds(r0, CHUNK), :]
        ybuf[pl.ds(CONV_HIST + r0, CHUNK), :] = a * jax.nn.sigmoid(g)
        return c

    lax.fori_loop(0, n_blk, glu_body, 0)

    first_tap = CONV_HIST - (CONV_KERNEL - 1)

    def body(r, c):
        r0 = pl.multiple_of(r * CHUNK, CHUNK)
        for gi in range(width // LANES):
            ls = slice(gi * LANES, (gi + 1) * LANES)
            win = ybuf[pl.ds(r0, CHUNK + CONV_HIST), ls]
            acc = _causal_taps(win, lambda k: w_ref[k:k + 1, ls], CONV_KERNEL, first_tap)
            y = acc + bias_ref[:, ls]
            mu = jnp.mean(y, axis=-1, keepdims=True)
            dlt = y - mu
            var = jnp.mean(dlt * dlt, axis=-1, keepdims=True)
            yn = dlt * lax.rsqrt(var + EPS) * lg_ref[:, ls] + lb_ref[:, ls]
            o_ref[pl.ds(r0, CHUNK), ls] = _silu(yn).astype(BF16)
        return c

    lax.fori_loop(0, n_blk, body, 0)


def _conformer_conv(p, w_dw, b_dw, ln_g, ln_b, nb, lp, width):
    r = p.shape[0]
    nch = lp // CHUNK
    t_rows = CHUNK * _divisor_tile(nch, CONV_CHUNKS, 1)
    nt = lp // t_rows
    assert width // LANES == CONV_GROUPS
    kern = functools.partial(_conv_kernel, t_rows=t_rows, width=width)
    vec = lambda b, t: (0, 0)
    return pl.pallas_call(
        kern,
        grid=(nb, nt),
        in_specs=[
            pl.BlockSpec((t_rows, width), lambda b, t: (b * nt + t, 0)),
            pl.BlockSpec((t_rows, width), lambda b, t: (b * nt + t, 1)),
            pl.BlockSpec((CONV_KERNEL, width), vec),
            pl.BlockSpec((1, width), vec),
            pl.BlockSpec((1, width), vec),
            pl.BlockSpec((1, width), vec),
        ],
        out_specs=pl.BlockSpec((t_rows, width), lambda b, t: (b * nt + t, 0)),
        out_shape=jax.ShapeDtypeStruct((r, width), BF16),
        scratch_shapes=[pltpu.VMEM((t_rows + CONV_HIST, width), F32)],
        compiler_params=_params(("arbitrary", "arbitrary")),
        name="conformer_conv",
    )(p, p, w_dw, b_dw.reshape(1, width), ln_g.reshape(1, width), ln_b.reshape(1, width))


def _delta_kernel(q_ref, k_ref, v_ref, z_ref, bdc_ref, bdr_ref, scw_ref, alr_ref, dtr_ref, alc_ref, dtc_ref,
                  ng_ref, o_ref, qbuf, kbuf, vbuf, s_sc, *, t_rows, pad, width, nh):
    t = pl.program_id(1)
    n_blk = t_rows // CHUNK
    bufs = (qbuf, kbuf, vbuf)
    srcs = (q_ref, k_ref, v_ref)

    @pl.when(t == 0)
    def _():
        s_sc[...] = jnp.zeros(s_sc.shape, F32)
        for buf in bufs:
            buf[0:SC_HIST, :] = jnp.zeros((SC_HIST, width), F32)

    @pl.when(t > 0)
    def _():
        for buf in bufs:
            buf[0:SC_HIST, :] = buf[t_rows:t_rows + SC_HIST, :]

    def copy_body(r, c):
        r0 = pl.multiple_of(r * CHUNK, CHUNK)
        for buf, src in zip(bufs, srcs):
            buf[pl.ds(SC_HIST + r0, CHUNK), :] = src[pl.ds(r0, CHUNK), :]
        return c

    lax.fori_loop(0, n_blk, copy_body, 0)

    ri = lax.broadcasted_iota(jnp.int32, (CHUNK, CHUNK), 0)
    ci = lax.broadcasted_iota(jnp.int32, (CHUNK, CHUNK), 1)
    incl = ri >= ci
    strict = ri > ci
    tril = incl.astype(F32)
    triu = (ri <= ci).astype(F32)
    eye = (ri == ci).astype(F32)
    eh = lax.broadcasted_iota(jnp.int32, (nh, width), 0)
    ec = lax.broadcasted_iota(jnp.int32, (nh, width), 1)
    expand = ((ec >= eh * HEAD_DIM) & (ec < (eh + 1) * HEAD_DIM)).astype(F32)
    first_tap = SC_HIST - (SHORT_CONV - 1)
    q_scale = HEAD_DIM ** -0.5

    def short_conv(buf, part, r0):
        win = buf[pl.ds(r0, CHUNK + SC_HIST), :]
        acc = _causal_taps(win, lambda k: scw_ref[k:k + 1, part * width:(part + 1) * width], SHORT_CONV, first_tap)
        return _silu(acc)

    def body(c, carry):
        r0 = pl.multiple_of(c * CHUNK, CHUNK)
        lrow = t * t_rows + r0 + lax.broadcasted_iota(jnp.int32, (CHUNK, 1), 0)
        lcol = t * t_rows + r0 + lax.broadcasted_iota(jnp.int32, (1, CHUNK), 1)
        bl = bdc_ref[pl.ds(r0, CHUNK), :]
        beta = jnp.where(lrow >= pad, jax.nn.sigmoid(bl[:, 0:nh]), 0.0)
        g_col = jnp.where(lrow >= pad, -jnp.exp(alr_ref[...]) * _softplus(bl[:, nh:2 * nh] + dtr_ref[...]), 0.0)
        br = bdr_ref[c]
        g_row = jnp.where(lcol >= pad, -jnp.exp(alc_ref[...]) * _softplus(br[nh:2 * nh, :] + dtc_ref[...]), 0.0)
        gc_col = _dot(tril, g_col, HI)
        gc_row = _dot(g_row, triu, HI)
        gc_full = _dot(gc_col, expand, HI)
        beta_full = _dot(beta, expand, HI)
        gl_full = gc_full[CHUNK - 1:CHUNK, :]
        eg_full = jnp.exp(gc_full)
        ekd_full = jnp.exp(gl_full - gc_full)
        glast_full = jnp.exp(gl_full)

        qc = short_conv(qbuf, 0, r0)
        kc = short_conv(kbuf, 1, r0)
        vc = short_conv(vbuf, 2, r0)
        zc = z_ref[pl.ds(r0, CHUNK), :]

        for h in range(nh):
            ls = slice(h * HEAD_DIM, (h + 1) * HEAD_DIM)
            qh = qc[:, ls]
            kh = kc[:, ls]
            vh = vc[:, ls]
            qh = qh * lax.rsqrt(jnp.sum(qh * qh, axis=-1, keepdims=True) + EPS) * q_scale
            kh = kh * lax.rsqrt(jnp.sum(kh * kh, axis=-1, keepdims=True) + EPS)
            bh = beta_full[:, ls]
            eg = eg_full[:, ls]
            diff = gc_full[:, h * HEAD_DIM:h * HEAD_DIM + CHUNK] - gc_row[h:h + 1, :]
            decay = jnp.where(incl, jnp.exp(jnp.where(incl, diff, 0.0)), 0.0)
            kb = kh * bh
            vb = vh * bh
            m = _dot_nt(kb, kh) * decay
            x = jnp.where(strict, -m, 0.0)
            ainv = eye + x
            xp = x
            n_sq = 1
            while 2 * n_sq < CHUNK:
                xp = _dot(xp, xp)
                ainv = ainv + _dot(ainv, xp)
                n_sq *= 2
            u = _dot(ainv, vb)
            w = _dot(ainv, kb * eg)
            intra = _dot_nt(qh, kh) * decay
            q_dec = qh * eg
            k_dec = kh * ekd_full[:, ls]
            s = s_sc[h]
            v_new = u - _dot(w, s)
            o = _dot(q_dec, s) + _dot(intra, v_new)
            s_sc[h] = s * glast_full[:, ls] + _dot(k_dec.T, v_new)
            on = o * lax.rsqrt(jnp.mean(o * o, axis=-1, keepdims=True) + EPS) * ng_ref[...]
            o_ref[pl.ds(r0, CHUNK), ls] = (on * _silu(zc[:, ls])).astype(BF16)
        return carry

    lax.fori_loop(0, n_blk, body, 0)


def _gated_deltanet(p, bd_col, bd_row3, sc_w, a_log, dt_bias, norm_g, nb, lp, pad, width, col0):
    r = p.shape[0]
    nh = width // HEAD_DIM
    nch = lp // CHUNK
    cpt = _divisor_tile(nch, DELTA_CHUNKS, 1)
    t_rows = CHUNK * cpt
    nt = lp // t_rows
    cb = col0 // width
    assert col0 % width == 0
    kern = functools.partial(_delta_kernel, t_rows=t_rows, pad=pad, width=width, nh=nh)
    vec = lambda b, t: (0, 0)
    part = lambda off: pl.BlockSpec((t_rows, width), lambda b, t: (b * nt + t, cb + off))
    return pl.pallas_call(
        kern,
        grid=(nb, nt),
        in_specs=[
            part(0), part(1), part(2), part(3),
            pl.BlockSpec((t_rows, 2 * nh), lambda b, t: (b * nt + t, 0)),
            pl.BlockSpec((cpt, 2 * nh, CHUNK), lambda b, t: (b * nt + t, 0, 0)),
            pl.BlockSpec((SHORT_CONV, 3 * width), vec),
            pl.BlockSpec((1, nh), vec),
            pl.BlockSpec((1, nh), vec),
            pl.BlockSpec((nh, 1), vec),
            pl.BlockSpec((nh, 1), vec),
            pl.BlockSpec((1, HEAD_DIM), vec),
        ],
        out_specs=pl.BlockSpec((t_rows, width), lambda b, t: (b * nt + t, 0)),
        out_shape=jax.ShapeDtypeStruct((r, width), BF16),
        scratch_shapes=[
            pltpu.VMEM((t_rows + SC_HIST, width), F32),
            pltpu.VMEM((t_rows + SC_HIST, width), F32),
            pltpu.VMEM((t_rows + SC_HIST, width), F32),
            pltpu.VMEM((nh, HEAD_DIM, HEAD_DIM), F32),
        ],
        compiler_params=_params(("arbitrary", "arbitrary")),
        name="gated_deltanet",
    )(p, p, p, p, bd_col, bd_row3, sc_w, a_log.reshape(1, nh), dt_bias.reshape(1, nh),
      a_log.reshape(nh, 1), dt_bias.reshape(nh, 1), norm_g.reshape(1, HEAD_DIM))


def _outproj_kernel(yc_ref, yd_ref, w1_ref, w2_ref, h_ref, o_ref):
    acc = _dot(yc_ref[...], w1_ref[...].astype(BF16))
    acc = acc + _dot(yd_ref[...], w2_ref[...].astype(BF16))
    o_ref[...] = h_ref[...] + acc


def _outproj(y_conv, y_delta, w_out, layer, h):
    r, d = h.shape
    kw = y_conv.shape[1]
    tm = _divisor_tile(r, OUTPROJ_TM, 16)
    tn = _divisor_tile(d, OUTPROJ_TN, LANES)
    return pl.pallas_call(
        _outproj_kernel,
        grid=(r // tm, d // tn),
        in_specs=[
            pl.BlockSpec((tm, kw), lambda i, j: (i, 0)),
            pl.BlockSpec((tm, kw), lambda i, j: (i, 0)),
            pl.BlockSpec((None, kw, tn), lambda i, j: (layer, 0, j)),
            pl.BlockSpec((None, kw, tn), lambda i, j: (layer, 1, j)),
            pl.BlockSpec((tm, tn), lambda i, j: (i, j)),
        ],
        out_specs=pl.BlockSpec((tm, tn), lambda i, j: (i, j)),
        out_shape=jax.ShapeDtypeStruct((r, d), F32),
        compiler_params=_params(("arbitrary", "arbitrary")),
        name="outproj",
    )(y_conv, y_delta, w_out, w_out, h)


def _first_argmax(vals, iota, n):
    m = jnp.max(vals, axis=-1, keepdims=True)
    idx = jnp.min(jnp.where(vals == m, iota, n), axis=-1, keepdims=True)
    return m, idx


def _router_kernel(h_ref, g_ref, w_ref, b_ref, u_ref, comb_ref):
    x = h_ref[...]
    ms = jnp.mean(x * x, axis=-1, keepdims=True)
    u = x * lax.rsqrt(ms + EPS) * g_ref[...]
    u_ref[...] = u.astype(BF16)
    logits = _dot(u, w_ref[...], HI) + b_ref[...]
    tm = x.shape[0]
    glog = logits[:, 0:N_GROUPS]
    elog = logits[:, N_GROUPS:N_GROUPS + N_EXPERTS]
    gi = lax.broadcasted_iota(jnp.int32, (tm, N_GROUPS), 1)
    gmax, gsel = _first_argmax(glog, gi, N_GROUPS)
    p_group = 1.0 / jnp.sum(jnp.exp(glog - gmax), axis=-1, keepdims=True)
    ei = lax.broadcasted_iota(jnp.int32, (tm, N_EXPERTS), 1)
    in_group = (ei >= gsel * EXPERTS_PER_GROUP) & (ei < (gsel + 1) * EXPERTS_PER_GROUP)
    neg = jnp.float32(-jnp.inf)
    cand = jnp.where(in_group, elog, neg)
    m1, i1 = _first_argmax(cand, ei, N_EXPERTS)
    cand2 = jnp.where(ei == i1, neg, cand)
    m2, i2 = _first_argmax(cand2, ei, N_EXPERTS)
    e2 = jnp.exp(m2 - m1)
    w1 = p_group / (1.0 + e2)
    w2 = p_group * e2 / (1.0 + e2)
    comb_ref[...] = jnp.where(ei == i1, w1, 0.0) + jnp.where(ei == i2, w2, 0.0)


def _router(h, norm_g, w_group, b_group, w_router, b_router):
    r, d = h.shape
    tm = _divisor_tile(r, ROUTER_TM, 16)
    w = jnp.concatenate([w_group, w_router], axis=1)
    b = jnp.concatenate([b_group, b_router]).reshape(1, -1)
    nl = w.shape[1]
    return pl.pallas_call(
        _router_kernel,
        grid=(r // tm,),
        in_specs=[
            pl.BlockSpec((tm, d), lambda i: (i, 0)),
            pl.BlockSpec((1, d), lambda i: (0, 0)),
            pl.BlockSpec((d, nl), lambda i: (0, 0)),
            pl.BlockSpec((1, nl), lambda i: (0, 0)),
        ],
        out_specs=[
            pl.BlockSpec((tm, d), lambda i: (i, 0)),
            pl.BlockSpec((tm, N_EXPERTS), lambda i: (i, 0)),
        ],
        out_shape=[
            jax.ShapeDtypeStruct((r, d), BF16),
            jax.ShapeDtypeStruct((r, N_EXPERTS), F32),
        ],
        compiler_params=_params(("arbitrary",)),
        name="router",
    )(h, norm_g.reshape(1, d), w, b)


def _moe_dense_kernel(u_ref, comb_ref, wg_ref, wu_ref, wd_ref, h_ref, o_ref):
    e = pl.program_id(1)

    @pl.when(e == 0)
    def _():
        o_ref[...] = h_ref[...]

    x = u_ref[...]
    hg = _dot(x, wg_ref[...].astype(BF16))
    hu = _dot(x, wu_ref[...].astype(BF16))
    comb = comb_ref[...]
    lane = lax.broadcasted_iota(jnp.int32, comb.shape, 1)
    ce = jnp.sum(jnp.where(lane == e, comb, 0.0), axis=-1, keepdims=True)
    hid = (_silu(hg) * hu * ce).astype(BF16)
    o_ref[...] += _dot(hid, wd_ref[...].astype(BF16))


def _moe_dense(u, comb, w_gate, w_up, w_down, layer, h):
    r, d = h.shape
    _, ne, _, f = w_gate.shape
    tm = _divisor_tile(r, MOE_TM, 16)
    return pl.pallas_call(
        _moe_dense_kernel,
        grid=(r // tm, ne),
        in_specs=[
            pl.BlockSpec((tm, d), lambda i, e: (i, 0)),
            pl.BlockSpec((tm, ne), lambda i, e: (i, 0)),
            pl.BlockSpec((None, None, d, f), lambda i, e: (layer, e, 0, 0)),
            pl.BlockSpec((None, None, d, f), lambda i, e: (layer, e, 0, 0)),
            pl.BlockSpec((None, None, f, d), lambda i, e: (layer, e, 0, 0)),
            pl.BlockSpec((tm, d), lambda i, e: (i, 0)),
        ],
        out_specs=pl.BlockSpec((tm, d), lambda i, e: (i, 0)),
        out_shape=jax.ShapeDtypeStruct((r, d), F32),
        compiler_params=_params(("arbitrary", "arbitrary")),
        name="moe_dense",
    )(u, comb, w_gate, w_up, w_down, h)


def _final_kernel(h_ref, g_ref, o_ref):
    x = h_ref[...]
    ms = jnp.mean(x * x, axis=-1, keepdims=True)
    o_ref[0] = x * lax.rsqrt(ms + EPS) * g_ref[...]


def _final_norm(h, norm_g, nb, lp, seq, skip):
    d = h.shape[1]
    t_rows = FINAL_T
    assert skip % t_rows == 0 and seq % t_rows == 0 and lp % t_rows == 0
    per_b, off = lp // t_rows, skip // t_rows
    return pl.pallas_call(
        _final_kernel,
        grid=(nb, seq // t_rows),
        in_specs=[
            pl.BlockSpec((t_rows, d), lambda b, t: (b * per_b + off + t, 0)),
            pl.BlockSpec((1, d), lambda b, t: (0, 0)),
        ],
        out_specs=pl.BlockSpec((1, t_rows, d), lambda b, t: (b, t, 0)),
        out_shape=jax.ShapeDtypeStruct((nb, seq, d), F32),
        compiler_params=_params(("arbitrary", "arbitrary")),
        name="final_norm",
    )(h, norm_g.reshape(1, d))


def kernel(x, meta, attn_norm, w_in, conv_dw_w, conv_dw_b, conv_ln_g, conv_ln_b, short_conv_w, a_log, dt_bias,
           delta_norm_g, w_out, ffn_norm, w_group, b_group, w_router, b_router, w_gate, w_up, w_down, final_norm):
    nb, seq, d = x.shape
    depth = w_in.shape[0]
    conv_w = conv_dw_w.shape[2]
    delta_w = short_conv_w.shape[2] // 3
    nh = delta_w // HEAD_DIM
    n_main = 2 * conv_w + 4 * delta_w
    assert w_in.shape[2] == n_main + 2 * nh and conv_w == delta_w
    ln = N_META + seq
    pad = (-ln) % CHUNK
    lp = ln + pad
    skip = pad + N_META

    meta_b = jnp.broadcast_to(meta[None].astype(x.dtype), (nb, N_META, d))
    h = jnp.concatenate([jnp.zeros((nb, pad, d), x.dtype), meta_b, x], axis=1).reshape(nb * lp, d)

    for l in range(depth):
        p, bd_col, bd_row = _inproj(h, attn_norm[l], w_in, l, n_main, lp, pad, nb)
        bd_row3 = bd_row.reshape(2 * nh, nb * lp // CHUNK, CHUNK).transpose(1, 0, 2)
        y_conv = _conformer_conv(p, conv_dw_w[l], conv_dw_b[l], conv_ln_g[l], conv_ln_b[l], nb, lp, conv_w)
        y_delta = _gated_deltanet(p, bd_col, bd_row3, short_conv_w[l], a_log[l], dt_bias[l], delta_norm_g[l],
                                  nb, lp, pad, delta_w, 2 * conv_w)
        h = _outproj(y_conv, y_delta, w_out, l, h)
        u, comb = _router(h, ffn_norm[l], w_group[l], b_group[l], w_router[l], b_router[l])
        h = _moe_dense(u, comb, w_gate, w_up, w_down, l, h)
    return _final_norm(h, final_norm, nb, lp, seq, skip)
```

```python
import functools

import jax
import jax.numpy as jnp
from jax import lax
from jax.experimental import pallas as pl
from jax.experimental.pallas import tpu as pltpu

F32 = jnp.float32
BF16 = jnp.bfloat16
HI = lax.Precision.HIGHEST

EPS = 1e-6
CHUNK = 64
N_META = 16
CONV_GROUPS = 8
CONV_KERNEL = 31
HEAD_DIM = 128
SHORT_CONV = 4
N_GROUPS = 4
EXPERTS_PER_GROUP = 8
N_EXPERTS = N_GROUPS * EXPERTS_PER_GROUP
TOP_K = 2
ROUTE_COLS = 8
LANES = 128
SUBLANES = 8
CONV_HIST = 32
SC_HIST = 8
VMEM_LIMIT = 56 * 1024 * 1024

INPROJ_TM, INPROJ_TN = 640, 512
CONV_CHUNKS = 13
DELTA_CHUNKS = 5
OUTPROJ_TM, OUTPROJ_TN = 832, 512
ROUTER_TM = 640
EXPERT_TM = 256
DISPATCH_T = 640
COMBINE_T = 640
DMA_UNROLL = 8
FINAL_T = 64


def _divisor_tile(n, cap, mult):
    best = None
    for t in range(mult, min(n, cap) + 1, mult):
        if n % t == 0:
            best = t
    if best is None:
        raise ValueError(f"no tile for n={n} cap={cap} mult={mult}")
    return best


def _params(sem):
    return pltpu.CompilerParams(dimension_semantics=sem, vmem_limit_bytes=VMEM_LIMIT)


def _dot(a, b, precision=None):
    return jnp.dot(a, b, preferred_element_type=F32, precision=precision)


def _dot_nt(a, b, precision=None):
    return lax.dot_general(a, b, (((1,), (1,)), ((), ())), preferred_element_type=F32, precision=precision)


def _silu(x):
    return x * jax.nn.sigmoid(x)


def _softplus(x):
    return jnp.maximum(x, 0.0) + jnp.log1p(jnp.exp(-jnp.abs(x)))


def _causal_taps(win, tap_w, n_taps, first_tap):
    acc = jnp.zeros((CHUNK,) + win.shape[1:], F32)
    for res in range(SUBLANES):
        offs = [o for o in range(first_tap, first_tap + n_taps) if o % SUBLANES == res]
        if not offs:
            continue
        shifted = win[res:, :] if res else win
        for o in offs:
            a = o - res
            acc = acc + tap_w(o - first_tap) * shifted[a:a + CHUNK, :]
    return acc


def _inproj_kernel(h_ref, g_ref, w_ref, wbd_ref, wbdt_ref, p_ref, bdc_ref, bdr_ref, u_sc, *, tm, lp, pad, nb):
    i = pl.program_id(0)
    j = pl.program_id(1)

    @pl.when(j == 0)
    def _():
        x = h_ref[...]
        ms = jnp.mean(x * x, axis=-1, keepdims=True)
        u = x * lax.rsqrt(ms + EPS) * g_ref[...]
        row = i * tm + lax.broadcasted_iota(jnp.int32, (tm, 1), 0)
        valid = (row >= pad) & (row < lp)
        for b in range(1, nb):
            valid = valid | ((row >= b * lp + pad) & (row < (b + 1) * lp))
        ub = jnp.where(valid, u, 0.0).astype(BF16)
        u_sc[...] = ub
        bdc_ref[...] = _dot(ub, wbd_ref[...].astype(BF16))
        bdr_ref[...] = _dot_nt(wbdt_ref[...].astype(BF16), ub)

    p_ref[...] = _dot(u_sc[...], w_ref[...].astype(BF16))


def _inproj(h, norm_g, w_in, layer, n_main, lp, pad, nb):
    r, d = h.shape
    n_bd = w_in.shape[2] - n_main
    tm = _divisor_tile(r, INPROJ_TM, LANES) if r > LANES else r
    tn = _divisor_tile(n_main, INPROJ_TN, LANES)
    w_bd = w_in[layer, :, n_main:]
    kern = functools.partial(_inproj_kernel, tm=tm, lp=lp, pad=pad, nb=nb)
    return pl.pallas_call(
        kern,
        grid=(r // tm, n_main // tn),
        in_specs=[
            pl.BlockSpec((tm, d), lambda i, j: (i, 0)),
            pl.BlockSpec((1, d), lambda i, j: (0, 0)),
            pl.BlockSpec((None, d, tn), lambda i, j: (layer, 0, j)),
            pl.BlockSpec((d, n_bd), lambda i, j: (0, 0)),
            pl.BlockSpec((n_bd, d), lambda i, j: (0, 0)),
        ],
        out_specs=[
            pl.BlockSpec((tm, tn), lambda i, j: (i, j)),
            pl.BlockSpec((tm, n_bd), lambda i, j: (i, 0)),
            pl.BlockSpec((n_bd, tm), lambda i, j: (0, i)),
        ],
        out_shape=[
            jax.ShapeDtypeStruct((r, n_main), F32),
            jax.ShapeDtypeStruct((r, n_bd), F32),
            jax.ShapeDtypeStruct((n_bd, r), F32),
        ],
        scratch_shapes=[pltpu.VMEM((tm, d), BF16)],
        compiler_params=_params(("arbitrary", "arbitrary")),
        name="inproj",
    )(h, norm_g.reshape(1, d), w_in, w_bd, w_bd.T)


def _conv_kernel(a_ref, b_ref, w_ref, bias_ref, lg_ref, lb_ref, o_ref, ybuf, *, t_rows, width):
    t = pl.program_id(1)
    n_blk = t_rows // CHUNK

    @pl.when(t == 0)
    def _():
        ybuf[0:CONV_HIST, :] = jnp.zeros((CONV_HIST, width), F32)

    @pl.when(t > 0)
    def _():
        ybuf[0:CONV_HIST, :] = ybuf[t_rows:t_rows + CONV_HIST, :]

    def glu_body(r, c):
        r0 = pl.multiple_of(r * CHUNK, CHUNK)
        a = a_ref[pl.ds(r0, CHUNK), :]
        g = b_ref[pl.ds(r0, CHUNK), :]
        ybuf[pl.ds(CONV_HIST + r0, CHUNK), :] = a * jax.nn.sigmoid(g)
        return c

    lax.fori_loop(0, n_blk, glu_body, 0)

    first_tap = CONV_HIST - (CONV_KERNEL - 1)

    def body(r, c):
        r0 = pl.multiple_of(r * CHUNK, CHUNK)
        for gi in range(width // LANES):
            ls = slice(gi * LANES, (gi + 1) * LANES)
            win = ybuf[pl.ds(r0, CHUNK + CONV_HIST), ls]
            acc = _causal_taps(win, lambda k: w_ref[k:k + 1, ls], CONV_KERNEL, first_tap)
            y = acc + bias_ref[:, ls]
            mu = jnp.mean(y, axis=-1, keepdims=True)
            dlt = y - mu
            var = jnp.mean(dlt * dlt, axis=-1, keepdims=True)
            yn = dlt * lax.rsqrt(var + EPS) * lg_ref[:, ls] + lb_ref[:, ls]
            o_ref[pl.ds(r0, CHUNK), ls] = _silu(yn).astype(BF16)
        return c

    lax.fori_loop(0, n_blk, body, 0)


def _conformer_conv(p, w_dw, b_dw, ln_g, ln_b, nb, lp, width):
    r = p.shape[0]
    nch = lp // CHUNK
    t_rows = CHUNK * _divisor_tile(nch, CONV_CHUNKS, 1)
    nt = lp // t_rows
    assert width // LANES == CONV_GROUPS
    kern = functools.partial(_conv_kernel, t_rows=t_rows, width=width)
    vec = lambda b, t: (0, 0)
    return pl.pallas_call(
        kern,
        grid=(nb, nt),
        in_specs=[
            pl.BlockSpec((t_rows, width), lambda b, t: (b * nt + t, 0)),
            pl.BlockSpec((t_rows, width), lambda b, t: (b * nt + t, 1)),
            pl.BlockSpec((CONV_KERNEL, width), vec),
            pl.BlockSpec((1, width), vec),
            pl.BlockSpec((1, width), vec),
            pl.BlockSpec((1, width), vec),
        ],
        out_specs=pl.BlockSpec((t_rows, width), lambda b, t: (b * nt + t, 0)),
        out_shape=jax.ShapeDtypeStruct((r, width), BF16),
        scratch_shapes=[pltpu.VMEM((t_rows + CONV_HIST, width), F32)],
        compiler_params=_params(("arbitrary", "arbitrary")),
        name="conformer_conv",
    )(p, p, w_dw, b_dw.reshape(1, width), ln_g.reshape(1, width), ln_b.reshape(1, width))


def _delta_kernel(q_ref, k_ref, v_ref, z_ref, bdc_ref, bdr_ref, scw_ref, alr_ref, dtr_ref, alc_ref, dtc_ref,
                  ng_ref, o_ref, qbuf, kbuf, vbuf, s_sc, *, t_rows, pad, width, nh):
    t = pl.program_id(1)
    n_blk = t_rows // CHUNK
    bufs = (qbuf, kbuf, vbuf)
    srcs = (q_ref, k_ref, v_ref)

    @pl.when(t == 0)
    def _():
        s_sc[...] = jnp.zeros(s_sc.shape, F32)
        for buf in bufs:
            buf[0:SC_HIST, :] = jnp.zeros((SC_HIST, width), F32)

    @pl.when(t > 0)
    def _():
        for buf in bufs:
            buf[0:SC_HIST, :] = buf[t_rows:t_rows + SC_HIST, :]

    def copy_body(r, c):
        r0 = pl.multiple_of(r * CHUNK, CHUNK)
        for buf, src in zip(bufs, srcs):
            buf[pl.ds(SC_HIST + r0, CHUNK), :] = src[pl.ds(r0, CHUNK), :]
        return c

    lax.fori_loop(0, n_blk, copy_body, 0)

    ri = lax.broadcasted_iota(jnp.int32, (CHUNK, CHUNK), 0)
    ci = lax.broadcasted_iota(jnp.int32, (CHUNK, CHUNK), 1)
    incl = ri >= ci
    strict = ri > ci
    tril = incl.astype(F32)
    triu = (ri <= ci).astype(F32)
    eye = (ri == ci).astype(F32)
    eh = lax.broadcasted_iota(jnp.int32, (nh, width), 0)
    ec = lax.broadcasted_iota(jnp.int32, (nh, width), 1)
    expand = ((ec >= eh * HEAD_DIM) & (ec < (eh + 1) * HEAD_DIM)).astype(F32)
    first_tap = SC_HIST - (SHORT_CONV - 1)
    q_scale = HEAD_DIM ** -0.5

    def short_conv(buf, part, r0):
        win = buf[pl.ds(r0, CHUNK + SC_HIST), :]
        acc = _causal_taps(win, lambda k: scw_ref[k:k + 1, part * width:(part + 1) * width], SHORT_CONV, first_tap)
        return _silu(acc)

    def body(c, carry):
        r0 = pl.multiple_of(c * CHUNK, CHUNK)
        lrow = t * t_rows + r0 + lax.broadcasted_iota(jnp.int32, (CHUNK, 1), 0)
        lcol = t * t_rows + r0 + lax.broadcasted_iota(jnp.int32, (1, CHUNK), 1)
        bl = bdc_ref[pl.ds(r0, CHUNK), :]
        beta = jnp.where(lrow >= pad, jax.nn.sigmoid(bl[:, 0:nh]), 0.0)
        g_col = jnp.where(lrow >= pad, -jnp.exp(alr_ref[...]) * _softplus(bl[:, nh:2 * nh] + dtr_ref[...]), 0.0)
        br = bdr_ref[c]
        g_row = jnp.where(lcol >= pad, -jnp.exp(alc_ref[...]) * _softplus(br[nh:2 * nh, :] + dtc_ref[...]), 0.0)
        gc_col = _dot(tril, g_col, HI)
        gc_row = _dot(g_row, triu, HI)
        gc_full = _dot(gc_col, expand, HI)
        beta_full = _dot(beta, expand, HI)
        gl_full = gc_full[CHUNK - 1:CHUNK, :]
        eg_full = jnp.exp(gc_full)
        ekd_full = jnp.exp(gl_full - gc_full)
        glast_full = jnp.exp(gl_full)

        qc = short_conv(qbuf, 0, r0)
        kc = short_conv(kbuf, 1, r0)
        vc = short_conv(vbuf, 2, r0)
        zc = z_ref[pl.ds(r0, CHUNK), :]

        for h in range(nh):
            ls = slice(h * HEAD_DIM, (h + 1) * HEAD_DIM)
            qh = qc[:, ls]
            kh = kc[:, ls]
            vh = vc[:, ls]
            qh = qh * lax.rsqrt(jnp.sum(qh * qh, axis=-1, keepdims=True) + EPS) * q_scale
            kh = kh * lax.rsqrt(jnp.sum(kh * kh, axis=-1, keepdims=True) + EPS)
            bh = beta_full[:, ls]
            eg = eg_full[:, ls]
            diff = gc_full[:, h * HEAD_DIM:h * HEAD_DIM + CHUNK] - gc_row[h:h + 1, :]
            decay = jnp.where(incl, jnp.exp(jnp.where(incl, diff, 0.0)), 0.0)
            kb = kh * bh
            vb = vh * bh
            m = _dot_nt(kb, kh) * decay
            x = jnp.where(strict, -m, 0.0)
            ainv = eye + x
            xp = x
            n_sq = 1
            while 2 * n_sq < CHUNK:
                xp = _dot(xp, xp)
                ainv = ainv + _dot(ainv, xp)
                n_sq *= 2
            u = _dot(ainv, vb)
            w = _dot(ainv, kb * eg)
            intra = _dot_nt(qh, kh) * decay
            q_dec = qh * eg
            k_dec = kh * ekd_full[:, ls]
            s = s_sc[h]
            v_new = u - _dot(w, s)
            o = _dot(q_dec, s) + _dot(intra, v_new)
            s_sc[h] = s * glast_full[:, ls] + _dot(k_dec.T, v_new)
            on = o * lax.rsqrt(jnp.mean(o * o, axis=-1, keepdims=True) + EPS) * ng_ref[...]
            o_ref[pl.ds(r0, CHUNK), ls] = (on * _silu(zc[:, ls])).astype(BF16)
        return carry

    lax.fori_loop(0, n_blk, body, 0)


def _gated_deltanet(p, bd_col, bd_row3, sc_w, a_log, dt_bias, norm_g, nb, lp, pad, width, col0):
    r = p.shape[0]
    nh = width // HEAD_DIM
    nch = lp // CHUNK
    cpt = _divisor_tile(nch, DELTA_CHUNKS, 1)
    t_rows = CHUNK * cpt
    nt = lp // t_rows
    cb = col0 // width
    assert col0 % width == 0
    kern = functools.partial(_delta_kernel, t_rows=t_rows, pad=pad, width=width, nh=nh)
    vec = lambda b, t: (0, 0)
    part = lambda off: pl.BlockSpec((t_rows, width), lambda b, t: (b * nt + t, cb + off))
    return pl.pallas_call(
        kern,
        grid=(nb, nt),
        in_specs=[
            part(0), part(1), part(2), part(3),
            pl.BlockSpec((t_rows, 2 * nh), lambda b, t: (b * nt + t, 0)),
            pl.BlockSpec((cpt, 2 * nh, CHUNK), lambda b, t: (b * nt + t, 0, 0)),
            pl.BlockSpec((SHORT_CONV, 3 * width), vec),
            pl.BlockSpec((1, nh), vec),
            pl.BlockSpec((1, nh), vec),
            pl.BlockSpec((nh, 1), vec),
            pl.BlockSpec((nh, 1), vec),
            pl.BlockSpec((1, HEAD_DIM), vec),
        ],
        out_specs=pl.BlockSpec((t_rows, width), lambda b, t: (b * nt + t, 0)),
        out_shape=jax.ShapeDtypeStruct((r, width), BF16),
        scratch_shapes=[
            pltpu.VMEM((t_rows + SC_HIST, width), F32),
            pltpu.VMEM((t_rows + SC_HIST, width), F32),
            pltpu.VMEM((t_rows + SC_HIST, width), F32),
            pltpu.VMEM((nh, HEAD_DIM, HEAD_DIM), F32),
        ],
        compiler_params=_params(("arbitrary", "arbitrary")),
        name="gated_deltanet",
    )(p, p, p, p, bd_col, bd_row3, sc_w, a_log.reshape(1, nh), dt_bias.reshape(1, nh),
      a_log.reshape(nh, 1), dt_bias.reshape(nh, 1), norm_g.reshape(1, HEAD_DIM))


def _outproj_kernel(yc_ref, yd_ref, w1_ref, w2_ref, h_ref, o_ref):
    acc = _dot(yc_ref[...], w1_ref[...].astype(BF16))
    acc = acc + _dot(yd_ref[...], w2_ref[...].astype(BF16))
    o_ref[...] = h_ref[...] + acc


def _outproj(y_conv, y_delta, w_out, layer, h):
    r, d = h.shape
    kw = y_conv.shape[1]
    tm = _divisor_tile(r, OUTPROJ_TM, 16)
    tn = _divisor_tile(d, OUTPROJ_TN, LANES)
    return pl.pallas_call(
        _outproj_kernel,
        grid=(r // tm, d // tn),
        in_specs=[
            pl.BlockSpec((tm, kw), lambda i, j: (i, 0)),
            pl.BlockSpec((tm, kw), lambda i, j: (i, 0)),
            pl.BlockSpec((None, kw, tn), lambda i, j: (layer, 0, j)),
            pl.BlockSpec((None, kw, tn), lambda i, j: (layer, 1, j)),
            pl.BlockSpec((tm, tn), lambda i, j: (i, j)),
        ],
        out_specs=pl.BlockSpec((tm, tn), lambda i, j: (i, j)),
        out_shape=jax.ShapeDtypeStruct((r, d), F32),
        compiler_params=_params(("arbitrary", "arbitrary")),
        name="outproj",
    )(y_conv, y_delta, w_out, w_out, h)


def _first_argmax(vals, iota, n):
    m = jnp.max(vals, axis=-1, keepdims=True)
    idx = jnp.min(jnp.where(vals == m, iota, n), axis=-1, keepdims=True)
    return m, idx


def _router_kernel(h_ref, g_ref, w_ref, b_ref, u_ref, route_ref, cnt_ref, carry_sc):
    @pl.when(pl.program_id(0) == 0)
    def _():
        carry_sc[...] = jnp.zeros(carry_sc.shape, F32)

    x = h_ref[...]
    ms = jnp.mean(x * x, axis=-1, keepdims=True)
    u = x * lax.rsqrt(ms + EPS) * g_ref[...]
    u_ref[...] = u
    logits = _dot(u, w_ref[...], HI) + b_ref[...]
    tm = x.shape[0]
    glog = logits[:, 0:N_GROUPS]
    elog = logits[:, N_GROUPS:N_GROUPS + N_EXPERTS]
    gi = lax.broadcasted_iota(jnp.int32, (tm, N_GROUPS), 1)
    gmax, gsel = _first_argmax(glog, gi, N_GROUPS)
    p_group = 1.0 / jnp.sum(jnp.exp(glog - gmax), axis=-1, keepdims=True)
    ei = lax.broadcasted_iota(jnp.int32, (tm, N_EXPERTS), 1)
    in_group = (ei >= gsel * EXPERTS_PER_GROUP) & (ei < (gsel + 1) * EXPERTS_PER_GROUP)
    neg = jnp.float32(-jnp.inf)
    cand = jnp.where(in_group, elog, neg)
    m1, i1 = _first_argmax(cand, ei, N_EXPERTS)
    cand2 = jnp.where(ei == i1, neg, cand)
    m2, i2 = _first_argmax(cand2, ei, N_EXPERTS)
    e2 = jnp.exp(m2 - m1)
    w1 = p_group / (1.0 + e2)
    w2 = p_group * e2 / (1.0 + e2)
    oh1 = (ei == i1).astype(F32)
    oh2 = (ei == i2).astype(F32)
    oh = oh1 + oh2
    ri = lax.broadcasted_iota(jnp.int32, (tm, tm), 0)
    ci = lax.broadcasted_iota(jnp.int32, (tm, tm), 1)
    before = _dot((ri > ci).astype(BF16), oh.astype(BF16)) + carry_sc[...]
    r1 = jnp.sum(before * oh1, axis=-1, keepdims=True)
    r2 = jnp.sum(before * oh2, axis=-1, keepdims=True)
    carry_sc[...] += jnp.sum(oh, axis=0, keepdims=True)
    cnt_ref[...] = carry_sc[...]
    li = lax.broadcasted_iota(jnp.int32, (tm, ROUTE_COLS), 1)
    rec = jnp.zeros((tm, ROUTE_COLS), F32)
    for k, col in enumerate((i1.astype(F32), i2.astype(F32), r1, r2, w1, w2)):
        rec = jnp.where(li == k, col, rec)
    route_ref[...] = rec


def _router(h, norm_g, w_group, b_group, w_router, b_router):
    r, d = h.shape
    tm = _divisor_tile(r, ROUTER_TM, 16)
    w = jnp.concatenate([w_group, w_router], axis=1)
    b = jnp.concatenate([b_group, b_router]).reshape(1, -1)
    nl = w.shape[1]
    return pl.pallas_call(
        _router_kernel,
        grid=(r // tm,),
        in_specs=[
            pl.BlockSpec((tm, d), lambda i: (i, 0)),
            pl.BlockSpec((1, d), lambda i: (0, 0)),
            pl.BlockSpec((d, nl), lambda i: (0, 0)),
            pl.BlockSpec((1, nl), lambda i: (0, 0)),
        ],
        out_specs=[
            pl.BlockSpec((tm, d), lambda i: (i, 0)),
            pl.BlockSpec((tm, ROUTE_COLS), lambda i: (i, 0)),
            pl.BlockSpec((1, N_EXPERTS), lambda i: (0, 0)),
        ],
        out_shape=[
            jax.ShapeDtypeStruct((r, d), F32),
            jax.ShapeDtypeStruct((r, ROUTE_COLS), F32),
            jax.ShapeDtypeStruct((1, N_EXPERTS), F32),
        ],
        scratch_shapes=[pltpu.VMEM((1, N_EXPERTS), F32)],
        compiler_params=_params(("arbitrary",)),
        name="router",
    )(h, norm_g.reshape(1, d), w, b)


def _dispatch_plan(route, cnt, tm, n_tiles):
    idx = route[:, 0:4].astype(jnp.int32)
    cnt = cnt[0].astype(jnp.int32)
    padded = ((cnt + tm - 1) // tm) * tm
    ends = jnp.cumsum(padded)
    off = ends - padded
    pos0 = off[idx[:, 0]] + idx[:, 2]
    pos1 = off[idx[:, 1]] + idx[:, 3]
    n_used = ends[-1] // tm
    tiles = jnp.arange(n_tiles, dtype=jnp.int32)
    tile_e = jnp.sum((tiles[:, None] * tm >= ends[None, :]).astype(jnp.int32), axis=1)
    tile_e = jnp.minimum(jnp.where(tiles < n_used, tile_e, tile_e[n_used - 1]), N_EXPERTS - 1)
    return pos0, pos1, tile_e, n_used.reshape(1)


def _dispatch_kernel(pos0_ref, pos1_ref, u_ref, xs_init_ref, xs_ref, sem, *, t_rows):
    del xs_init_ref
    base = pl.program_id(0) * t_rows

    def body(r, c):
        src = u_ref.at[pl.ds(r, 1)]
        pltpu.make_async_copy(src, xs_ref.at[pl.ds(pos0_ref[base + r], 1)], sem).start()
        pltpu.make_async_copy(src, xs_ref.at[pl.ds(pos1_ref[base + r], 1)], sem).start()
        return c

    lax.fori_loop(0, t_rows, body, 0, unroll=DMA_UNROLL)
    for _ in range(2):
        pltpu.make_async_copy(u_ref, xs_ref.at[pl.ds(0, t_rows)], sem).wait()


def _dispatch(u, pos0, pos1, n_slots):
    r, d = u.shape
    t_rows = _divisor_tile(r, DISPATCH_T, SUBLANES)
    kern = functools.partial(_dispatch_kernel, t_rows=t_rows)
    return pl.pallas_call(
        kern,
        grid_spec=pltpu.PrefetchScalarGridSpec(
            num_scalar_prefetch=2,
            grid=(r // t_rows,),
            in_specs=[
                pl.BlockSpec((t_rows, d), lambda i, p0, p1: (i, 0)),
                pl.BlockSpec(memory_space=pl.ANY),
            ],
            out_specs=pl.BlockSpec(memory_space=pl.ANY),
            scratch_shapes=[pltpu.SemaphoreType.DMA(())],
        ),
        out_shape=jax.ShapeDtypeStruct((n_slots, d), F32),
        input_output_aliases={3: 0},
        compiler_params=_params(("arbitrary",)),
        name="moe_dispatch",
    )(pos0, pos1, u, jnp.zeros((n_slots, d), F32))


def _expert_kernel(te_ref, nu_ref, x_ref, wg_ref, wu_ref, wd_ref, y_ref, wg_b, wu_b, wd_b):
    i = pl.program_id(0)
    new_expert = (i == 0) | (te_ref[i] != te_ref[jnp.maximum(i - 1, 0)])

    @pl.when(new_expert)
    def _():
        wg_b[...] = wg_ref[...].astype(BF16)
        wu_b[...] = wu_ref[...].astype(BF16)
        wd_b[...] = wd_ref[...].astype(BF16)

    @pl.when(i < nu_ref[0])
    def _():
        x = x_ref[...].astype(BF16)
        hg = _dot(x, wg_b[...])
        hu = _dot(x, wu_b[...])
        hid = (_silu(hg) * hu).astype(BF16)
        y_ref[...] = _dot(hid, wd_b[...])


def _experts(xs, tile_e, n_used, w_gate, w_up, w_down, layer, tm):
    n_slots, d = xs.shape
    f = w_gate.shape[3]
    n_tiles = n_slots // tm
    row = lambda i, te, nu: (jnp.minimum(i, nu[0] - 1), 0)
    wmap = lambda i, te, nu: (layer, te[i], 0, 0)
    return pl.pallas_call(
        _expert_kernel,
        grid_spec=pltpu.PrefetchScalarGridSpec(
            num_scalar_prefetch=2,
            grid=(n_tiles,),
            in_specs=[
                pl.BlockSpec((tm, d), row),
                pl.BlockSpec((None, None, d, f), wmap),
                pl.BlockSpec((None, None, d, f), wmap),
                pl.BlockSpec((None, None, f, d), wmap),
            ],
            out_specs=pl.BlockSpec((tm, d), row),
            scratch_shapes=[pltpu.VMEM((d, f), BF16), pltpu.VMEM((d, f), BF16), pltpu.VMEM((f, d), BF16)],
        ),
        out_shape=jax.ShapeDtypeStruct((n_slots, d), F32),
        input_output_aliases={2: 0},
        compiler_params=_params(("arbitrary",)),
        name="moe_experts",
    )(tile_e, n_used, xs, w_gate, w_up, w_down)


def _combine_kernel(pos0_ref, pos1_ref, h_ref, route_ref, ys_ref, o_ref, buf0, buf1, sem, *, t_rows):
    base = pl.program_id(0) * t_rows

    def body(r, c):
        pltpu.make_async_copy(ys_ref.at[pl.ds(pos0_ref[base + r], 1)], buf0.at[pl.ds(r, 1)], sem).start()
        pltpu.make_async_copy(ys_ref.at[pl.ds(pos1_ref[base + r], 1)], buf1.at[pl.ds(r, 1)], sem).start()
        return c

    lax.fori_loop(0, t_rows, body, 0, unroll=DMA_UNROLL)
    for buf in (buf0, buf1):
        pltpu.make_async_copy(ys_ref.at[pl.ds(0, t_rows)], buf, sem).wait()
    g = route_ref[...]
    o_ref[...] = h_ref[...] + g[:, 4:5] * buf0[...] + g[:, 5:6] * buf1[...]


def _combine(h, route, ys, pos0, pos1):
    r, d = h.shape
    t_rows = _divisor_tile(r, COMBINE_T, SUBLANES)
    kern = functools.partial(_combine_kernel, t_rows=t_rows)
    return pl.pallas_call(
        kern,
        grid_spec=pltpu.PrefetchScalarGridSpec(
            num_scalar_prefetch=2,
            grid=(r // t_rows,),
            in_specs=[
                pl.BlockSpec((t_rows, d), lambda i, p0, p1: (i, 0)),
                pl.BlockSpec((t_rows, ROUTE_COLS), lambda i, p0, p1: (i, 0)),
                pl.BlockSpec(memory_space=pl.ANY),
            ],
            out_specs=pl.BlockSpec((t_rows, d), lambda i, p0, p1: (i, 0)),
            scratch_shapes=[pltpu.VMEM((t_rows, d), F32), pltpu.VMEM((t_rows, d), F32), pltpu.SemaphoreType.DMA(())],
        ),
        out_shape=jax.ShapeDtypeStruct((r, d), F32),
        compiler_params=_params(("arbitrary",)),
        name="moe_combine",
    )(pos0, pos1, h, route, ys)


def _hier_moe(h, norm_g, w_group, b_group, w_router, b_router, w_gate, w_up, w_down, layer):
    r = h.shape[0]
    tm = EXPERT_TM
    n_tiles = -(-(TOP_K * r + N_EXPERTS * (tm - 1)) // tm)
    u, route, cnt = _router(h, norm_g, w_group, b_group, w_router, b_router)
    pos0, pos1, tile_e, n_used = _dispatch_plan(route, cnt, tm, n_tiles)
    xs = _dispatch(u, pos0, pos1, n_tiles * tm)
    ys = _experts(xs, tile_e, n_used, w_gate, w_up, w_down, layer, tm)
    return _combine(h, route, ys, pos0, pos1)


def _final_kernel(h_ref, g_ref, o_ref):
    x = h_ref[...]
    ms = jnp.mean(x * x, axis=-1, keepdims=True)
    o_ref[0] = x * lax.rsqrt(ms + EPS) * g_ref[...]


def _final_norm(h, norm_g, nb, lp, seq, skip):
    d = h.shape[1]
    t_rows = FINAL_T
    assert skip % t_rows == 0 and seq % t_rows == 0 and lp % t_rows == 0
    per_b, off = lp // t_rows, skip // t_rows
    return pl.pallas_call(
        _final_kernel,
        grid=(nb, seq // t_rows),
        in_specs=[
            pl.BlockSpec((t_rows, d), lambda b, t: (b * per_b + off + t, 0)),
            pl.BlockSpec((1, d), lambda b, t: (0, 0)),
        ],
        out_specs=pl.BlockSpec((1, t_rows, d), lambda b, t: (b, t, 0)),
        out_shape=jax.ShapeDtypeStruct((nb, seq, d), F32),
        compiler_params=_params(("arbitrary", "arbitrary")),
        name="final_norm",
    )(h, norm_g.reshape(1, d))


def kernel(x, meta, attn_norm, w_in, conv_dw_w, conv_dw_b, conv_ln_g, conv_ln_b, short_conv_w, a_log, dt_bias,
           delta_norm_g, w_out, ffn_norm, w_group, b_group, w_router, b_router, w_gate, w_up, w_down, final_norm):
    nb, seq, d = x.shape
    depth = w_in.shape[0]
    conv_w = conv_dw_w.shape[2]
    delta_w = short_conv_w.shape[2] // 3
    nh = delta_w // HEAD_DIM
    n_main = 2 * conv_w + 4 * delta_w
    assert w_in.shape[2] == n_main + 2 * nh and conv_w == delta_w
    ln = N_META + seq
    pad = (-ln) % CHUNK
    lp = ln + pad
    skip = pad + N_META

    meta_b = jnp.broadcast_to(meta[None].astype(x.dtype), (nb, N_META, d))
    h = jnp.concatenate([jnp.zeros((nb, pad, d), x.dtype), meta_b, x], axis=1).reshape(nb * lp, d)

    for l in range(depth):
        p, bd_col, bd_row = _inproj(h, attn_norm[l], w_in, l, n_main, lp, pad, nb)
        bd_row3 = bd_row.reshape(2 * nh, nb * lp // CHUNK, CHUNK).transpose(1, 0, 2)
        y_conv = _conformer_conv(p, conv_dw_w[l], conv_dw_b[l], conv_ln_g[l], conv_ln_b[l], nb, lp, conv_w)
        y_delta = _gated_deltanet(p, bd_col, bd_row3, short_conv_w[l], a_log[l], dt_bias[l], delta_norm_g[l],
                                  nb, lp, pad, delta_w, 2 * conv_w)
        h = _outproj(y_conv, y_delta, w_out, l, h)
        h = _hier_moe(h, ffn_norm[l], w_group[l], b_group[l], w_router[l], b_router[l], w_gate, w_up, w_down, l)
    return _final_norm(h, final_norm, nb, lp, seq, skip)
```

```python
import functools

import jax
import jax.numpy as jnp
from jax import lax
from jax.experimental import pallas as pl
from jax.experimental.pallas import tpu as pltpu

F32 = jnp.float32
BF16 = jnp.bfloat16
HI = lax.Precision.HIGHEST

EPS = 1e-6
CHUNK = 64
N_META = 16
CONV_GROUPS = 8
CONV_KERNEL = 31
HEAD_DIM = 128
SHORT_CONV = 4
N_GROUPS = 4
EXPERTS_PER_GROUP = 8
N_EXPERTS = N_GROUPS * EXPERTS_PER_GROUP
TOP_K = 2
ROUTE_COLS = 8
LANES = 128
SUBLANES = 8
CONV_HIST = 32
SC_HIST = 8
VMEM_LIMIT = 56 * 1024 * 1024

INPROJ_TM, INPROJ_TN = 640, 512
CONV_CHUNKS = 13
DELTA_CHUNKS = 5
OUTPROJ_TM, OUTPROJ_TN = 832, 512
ROUTER_TM = 640
EXPERT_TM = 256
DISPATCH_T = 640
COMBINE_T = 640
DMA_UNROLL = 8
FINAL_T = 512


def _divisor_tile(n, cap, mult):
    best = None
    for t in range(mult, min(n, cap) + 1, mult):
        if n % t == 0:
            best = t
    if best is None:
        raise ValueError(f"no tile for n={n} cap={cap} mult={mult}")
    return best


def _params(sem):
    return pltpu.CompilerParams(dimension_semantics=sem, vmem_limit_bytes=VMEM_LIMIT)


def _dot(a, b, precision=None):
    return jnp.dot(a, b, preferred_element_type=F32, precision=precision)


def _dot_nt(a, b, precision=None):
    return lax.dot_general(a, b, (((1,), (1,)), ((), ())), preferred_element_type=F32, precision=precision)


def _silu(x):
    return x * jax.nn.sigmoid(x)


def _softplus(x):
    return jnp.maximum(x, 0.0) + jnp.log1p(jnp.exp(-jnp.abs(x)))


def _dot_split(a, b, n_parts, split_lhs):
    x = a if split_lhs else b
    acc = None
    for _ in range(n_parts):
        piece = x.astype(BF16)
        term = _dot(piece, b) if split_lhs else _dot(a, piece)
        acc = term if acc is None else acc + term
        x = x - piece.astype(F32)
    return acc


def _causal_taps(win, tap_w, n_taps, first_tap):
    acc = jnp.zeros((CHUNK,) + win.shape[1:], F32)
    for res in range(SUBLANES):
        offs = [o for o in range(first_tap, first_tap + n_taps) if o % SUBLANES == res]
        if not offs:
            continue
        shifted = win[res:, :] if res else win
        for o in offs:
            a = o - res
            acc = acc + tap_w(o - first_tap) * shifted[a:a + CHUNK, :]
    return acc


def _inproj_kernel(h_ref, g_ref, w_ref, wbd_ref, p_ref, bdc_ref, bdr_ref, u_sc, *, tm, lp, pad, nb):
    i = pl.program_id(0)
    j = pl.program_id(1)

    @pl.when(j == 0)
    def _():
        x = h_ref[...]
        ms = jnp.mean(x * x, axis=-1, keepdims=True)
        u = x * lax.rsqrt(ms + EPS) * g_ref[...]
        row = i * tm + lax.broadcasted_iota(jnp.int32, (tm, 1), 0)
        valid = (row >= pad) & (row < lp)
        for b in range(1, nb):
            valid = valid | ((row >= b * lp + pad) & (row < (b + 1) * lp))
        ub = jnp.where(valid, u, 0.0).astype(BF16)
        u_sc[...] = ub
        wbd = wbd_ref[...].astype(BF16)
        bdc_ref[...] = _dot(ub, wbd)
        bdr_ref[...] = lax.dot_general(wbd, ub, (((0,), (1,)), ((), ())), preferred_element_type=F32)

    p_ref[...] = _dot(u_sc[...], w_ref[...].astype(BF16))


def _inproj(h, norm_g, w_in, layer, n_main, lp, pad, nb):
    r, d = h.shape
    n_bd = w_in.shape[2] - n_main
    tm = _divisor_tile(r, INPROJ_TM, LANES) if r > LANES else r
    tn = _divisor_tile(n_main, INPROJ_TN, LANES)
    w_bd = w_in[layer, :, n_main:]
    kern = functools.partial(_inproj_kernel, tm=tm, lp=lp, pad=pad, nb=nb)
    return pl.pallas_call(
        kern,
        grid=(r // tm, n_main // tn),
        in_specs=[
            pl.BlockSpec((tm, d), lambda i, j: (i, 0)),
            pl.BlockSpec((1, d), lambda i, j: (0, 0)),
            pl.BlockSpec((None, d, tn), lambda i, j: (layer, 0, j)),
            pl.BlockSpec((d, n_bd), lambda i, j: (0, 0)),
        ],
        out_specs=[
            pl.BlockSpec((tm, tn), lambda i, j: (i, j)),
            pl.BlockSpec((tm, n_bd), lambda i, j: (i, 0)),
            pl.BlockSpec((n_bd, tm), lambda i, j: (0, i)),
        ],
        out_shape=[
            jax.ShapeDtypeStruct((r, n_main), F32),
            jax.ShapeDtypeStruct((r, n_bd), F32),
            jax.ShapeDtypeStruct((n_bd, r), F32),
        ],
        scratch_shapes=[pltpu.VMEM((tm, d), BF16)],
        compiler_params=_params(("arbitrary", "arbitrary")),
        name="inproj",
    )(h, norm_g.reshape(1, d), w_in, w_bd)


def _conv_kernel(a_ref, b_ref, w_ref, bias_ref, lg_ref, lb_ref, o_ref, ybuf, *, t_rows, width):
    t = pl.program_id(1)
    n_blk = t_rows // CHUNK

    @pl.when(t == 0)
    def _():
        ybuf[0:CONV_HIST, :] = jnp.zeros((CONV_HIST, width), F32)

    @pl.when(t > 0)
    def _():
        ybuf[0:CONV_HIST, :] = ybuf[t_rows:t_rows + CONV_HIST, :]

    def glu_body(r, c):
        r0 = pl.multiple_of(r * CHUNK, CHUNK)
        a = a_ref[pl.ds(r0, CHUNK), :]
        g = b_ref[pl.ds(r0, CHUNK), :]
        ybuf[pl.ds(CONV_HIST + r0, CHUNK), :] = a * jax.nn.sigmoid(g)
        return c

    lax.fori_loop(0, n_blk, glu_body, 0)

    first_tap = CONV_HIST - (CONV_KERNEL - 1)

    def body(r, c):
        r0 = pl.multiple_of(r * CHUNK, CHUNK)
        for gi in range(width // LANES):
            ls = slice(gi * LANES, (gi + 1) * LANES)
            win = ybuf[pl.ds(r0, CHUNK + CONV_HIST), ls]
            acc = _causal_taps(win, lambda k: w_ref[k:k + 1, ls], CONV_KERNEL, first_tap)
            y = acc + bias_ref[:, ls]
            mu = jnp.mean(y, axis=-1, keepdims=True)
            dlt = y - mu
            var = jnp.mean(dlt * dlt, axis=-1, keepdims=True)
            yn = dlt * lax.rsqrt(var + EPS) * lg_ref[:, ls] + lb_ref[:, ls]
            o_ref[pl.ds(r0, CHUNK), ls] = _silu(yn).astype(BF16)
        return c

    lax.fori_loop(0, n_blk, body, 0)


def _conformer_conv(p, w_dw, b_dw, ln_g, ln_b, nb, lp, width):
    r = p.shape[0]
    nch = lp // CHUNK
    t_rows = CHUNK * _divisor_tile(nch, CONV_CHUNKS, 1)
    nt = lp // t_rows
    assert width // LANES == CONV_GROUPS
    kern = functools.partial(_conv_kernel, t_rows=t_rows, width=width)
    vec = lambda b, t: (0, 0)
    return pl.pallas_call(
        kern,
        grid=(nb, nt),
        in_specs=[
            pl.BlockSpec((t_rows, width), lambda b, t: (b * nt + t, 0)),
            pl.BlockSpec((t_rows, width), lambda b, t: (b * nt + t, 1)),
            pl.BlockSpec((CONV_KERNEL, width), vec),
            pl.BlockSpec((1, width), vec),
            pl.BlockSpec((1, width), vec),
            pl.BlockSpec((1, width), vec),
        ],
        out_specs=pl.BlockSpec((t_rows, width), lambda b, t: (b * nt + t, 0)),
        out_shape=jax.ShapeDtypeStruct((r, width), BF16),
        scratch_shapes=[pltpu.VMEM((t_rows + CONV_HIST, width), F32)],
        compiler_params=_params(("arbitrary", "arbitrary")),
        name="conformer_conv",
    )(p, p, w_dw, b_dw.reshape(1, width), ln_g.reshape(1, width), ln_b.reshape(1, width))


def _delta_kernel(q_ref, k_ref, v_ref, z_ref, bdc_ref, bdr_ref, scw_ref, alr_ref, dtr_ref, alc_ref, dtc_ref,
                  ng_ref, o_ref, qbuf, kbuf, vbuf, s_sc, *, t_rows, pad, width, nh):
    t = pl.program_id(1)
    n_blk = t_rows // CHUNK
    bufs = (qbuf, kbuf, vbuf)
    srcs = (q_ref, k_ref, v_ref)

    @pl.when(t == 0)
    def _():
        s_sc[...] = jnp.zeros(s_sc.shape, F32)
        for buf in bufs:
            buf[0:SC_HIST, :] = jnp.zeros((SC_HIST, width), F32)

    @pl.when(t > 0)
    def _():
        for buf in bufs:
            buf[0:SC_HIST, :] = buf[t_rows:t_rows + SC_HIST, :]

    def copy_body(r, c):
        r0 = pl.multiple_of(r * CHUNK, CHUNK)
        for buf, src in zip(bufs, srcs):
            buf[pl.ds(SC_HIST + r0, CHUNK), :] = src[pl.ds(r0, CHUNK), :]
        return c

    lax.fori_loop(0, n_blk, copy_body, 0)

    ri = lax.broadcasted_iota(jnp.int32, (CHUNK, CHUNK), 0)
    ci = lax.broadcasted_iota(jnp.int32, (CHUNK, CHUNK), 1)
    incl = ri >= ci
    strict = ri > ci
    tril = incl.astype(BF16)
    triu = (ri <= ci).astype(BF16)
    eye = (ri == ci).astype(F32)

    def head_expand(lanes):
        eh = lax.broadcasted_iota(jnp.int32, (nh, nh * lanes), 0)
        ec = lax.broadcasted_iota(jnp.int32, (nh, nh * lanes), 1)
        return ((ec >= eh * lanes) & (ec < (eh + 1) * lanes)).astype(BF16)

    expand_c = head_expand(CHUNK)
    expand_d = head_expand(HEAD_DIM)
    first_tap = SC_HIST - (SHORT_CONV - 1)
    q_scale = HEAD_DIM ** -0.5
    heads = range(nh)

    def short_conv(buf, part, r0):
        win = buf[pl.ds(r0, CHUNK + SC_HIST), :]
        acc = _causal_taps(win, lambda k: scw_ref[k:k + 1, part * width:(part + 1) * width], SHORT_CONV, first_tap)
        return _silu(acc)

    def body(c, carry):
        r0 = pl.multiple_of(c * CHUNK, CHUNK)
        lrow = t * t_rows + r0 + lax.broadcasted_iota(jnp.int32, (CHUNK, 1), 0)
        lcol = t * t_rows + r0 + lax.broadcasted_iota(jnp.int32, (1, CHUNK), 1)
        bl = bdc_ref[pl.ds(r0, CHUNK), :]
        br = bdr_ref[c]
        beta_col = jnp.where(lrow >= pad, jax.nn.sigmoid(bl[:, 0:nh]), 0.0)
        beta_row = jnp.where(lcol >= pad, jax.nn.sigmoid(br[0:nh, :]), 0.0)
        g_col = jnp.where(lrow >= pad, -jnp.exp(alr_ref[...]) * _softplus(bl[:, nh:2 * nh] + dtr_ref[...]), 0.0)
        g_row = jnp.where(lcol >= pad, -jnp.exp(alc_ref[...]) * _softplus(br[nh:2 * nh, :] + dtc_ref[...]), 0.0)
        gc_col = _dot_split(tril, g_col, 3, split_lhs=False)
        gc_row = _dot_split(g_row, triu, 3, split_lhs=True)
        eg_row = jnp.exp(gc_row)
        ekd_row = jnp.exp(gc_row[:, CHUNK - 1:CHUNK] - gc_row)
        gc_x = _dot_split(gc_col, expand_c, 3, split_lhs=True)
        beta_x = _dot_split(beta_col, expand_c, 2, split_lhs=True)
        eg_x = _dot_split(jnp.exp(gc_col), expand_d, 2, split_lhs=True)

        qc = short_conv(qbuf, 0, r0)
        kc = short_conv(kbuf, 1, r0)
        vc = short_conv(vbuf, 2, r0)
        zc = z_ref[pl.ds(r0, CHUNK), :]

        hd = [slice(h * HEAD_DIM, (h + 1) * HEAD_DIM) for h in heads]
        hc = [slice(h * CHUNK, (h + 1) * CHUNK) for h in heads]
        qn = [qc[:, s] * lax.rsqrt(jnp.sum(qc[:, s] * qc[:, s], axis=-1, keepdims=True) + EPS) * q_scale for s in hd]
        kn = [kc[:, s] * lax.rsqrt(jnp.sum(kc[:, s] * kc[:, s], axis=-1, keepdims=True) + EPS) for s in hd]
        kb16 = [k.astype(BF16) for k in kn]
        v16 = [vc[:, s].astype(BF16) for s in hd]
        decay = [jnp.where(incl, jnp.exp(jnp.where(incl, gc_x[:, hc[h]] - gc_row[h:h + 1, :], 0.0)), 0.0) for h in heads]
        kq = [_dot_nt(jnp.concatenate([kb16[h], qn[h].astype(BF16)], axis=0), kb16[h]) for h in heads]
        xp = [jnp.where(strict, -(kq[h][0:CHUNK] * beta_x[:, hc[h]] * decay[h]), 0.0) for h in heads]
        intra = [kq[h][CHUNK:2 * CHUNK] * decay[h] for h in heads]
        ainv = [eye + x for x in xp]
        n_sq = 1
        while 2 * n_sq < CHUNK:
            xp16 = [x.astype(BF16) for x in xp]
            xp = [_dot(x, x) for x in xp16]
            ainv = [a + _dot(a.astype(BF16), x.astype(BF16)) for a, x in zip(ainv, xp)]
            n_sq *= 2
        u = [_dot((ainv[h] * beta_row[h:h + 1, :]).astype(BF16), v16[h]) for h in heads]
        w = [_dot((ainv[h] * (beta_row[h:h + 1, :] * eg_row[h:h + 1, :])).astype(BF16), kb16[h]) for h in heads]
        q_dec = [qn[h] * eg_x[:, hd[h]] for h in heads]
        s_old = [s_sc[h] for h in heads]
        wq_s = [_dot(jnp.concatenate([w[h], q_dec[h]], axis=0).astype(BF16), s_old[h].astype(BF16)) for h in heads]
        v_new = [u[h] - wq_s[h][0:CHUNK] for h in heads]
        kd_t = [kn[h].T * ekd_row[h:h + 1, :] for h in heads]
        iv = [_dot(jnp.concatenate([intra[h], kd_t[h]], axis=0).astype(BF16), v_new[h].astype(BF16)) for h in heads]
        for h in heads:
            o = wq_s[h][CHUNK:2 * CHUNK] + iv[h][0:CHUNK]
            s_sc[h] = s_old[h] * eg_x[CHUNK - 1:CHUNK, hd[h]] + iv[h][CHUNK:CHUNK + HEAD_DIM]
            on = o * lax.rsqrt(jnp.mean(o * o, axis=-1, keepdims=True) + EPS) * ng_ref[...]
            o_ref[pl.ds(r0, CHUNK), hd[h]] = (on * _silu(zc[:, hd[h]])).astype(BF16)
        return carry

    lax.fori_loop(0, n_blk, body, 0)


def _gated_deltanet(p, bd_col, bd_row3, sc_w, a_log, dt_bias, norm_g, nb, lp, pad, width, col0):
    r = p.shape[0]
    nh = width // HEAD_DIM
    nch = lp // CHUNK
    cpt = _divisor_tile(nch, DELTA_CHUNKS, 1)
    t_rows = CHUNK * cpt
    nt = lp // t_rows
    cb = col0 // width
    assert col0 % width == 0
    kern = functools.partial(_delta_kernel, t_rows=t_rows, pad=pad, width=width, nh=nh)
    vec = lambda b, t: (0, 0)
    part = lambda off: pl.BlockSpec((t_rows, width), lambda b, t: (b * nt + t, cb + off))
    return pl.pallas_call(
        kern,
        grid=(nb, nt),
        in_specs=[
            part(0), part(1), part(2), part(3),
            pl.BlockSpec((t_rows, 2 * nh), lambda b, t: (b * nt + t, 0)),
            pl.BlockSpec((cpt, 2 * nh, CHUNK), lambda b, t: (b * nt + t, 0, 0)),
            pl.BlockSpec((SHORT_CONV, 3 * width), vec),
            pl.BlockSpec((1, nh), vec),
            pl.BlockSpec((1, nh), vec),
            pl.BlockSpec((nh, 1), vec),
            pl.BlockSpec((nh, 1), vec),
            pl.BlockSpec((1, HEAD_DIM), vec),
        ],
        out_specs=pl.BlockSpec((t_rows, width), lambda b, t: (b * nt + t, 0)),
        out_shape=jax.ShapeDtypeStruct((r, width), BF16),
        scratch_shapes=[
            pltpu.VMEM((t_rows + SC_HIST, width), F32),
            pltpu.VMEM((t_rows + SC_HIST, width), F32),
            pltpu.VMEM((t_rows + SC_HIST, width), F32),
            pltpu.VMEM((nh, HEAD_DIM, HEAD_DIM), F32),
        ],
        compiler_params=_params(("arbitrary", "arbitrary")),
        name="gated_deltanet",
    )(p, p, p, p, bd_col, bd_row3, sc_w, a_log.reshape(1, nh), dt_bias.reshape(1, nh),
      a_log.reshape(nh, 1), dt_bias.reshape(nh, 1), norm_g.reshape(1, HEAD_DIM))


def _outproj_kernel(yc_ref, yd_ref, w1_ref, w2_ref, h_ref, o_ref):
    acc = _dot(yc_ref[...], w1_ref[...].astype(BF16))
    acc = acc + _dot(yd_ref[...], w2_ref[...].astype(BF16))
    o_ref[...] = h_ref[...] + acc


def _outproj(y_conv, y_delta, w_out, layer, h):
    r, d = h.shape
    kw = y_conv.shape[1]
    tm = _divisor_tile(r, OUTPROJ_TM, 16)
    tn = _divisor_tile(d, OUTPROJ_TN, LANES)
    return pl.pallas_call(
        _outproj_kernel,
        grid=(r // tm, d // tn),
        in_specs=[
            pl.BlockSpec((tm, kw), lambda i, j: (i, 0)),
            pl.BlockSpec((tm, kw), lambda i, j: (i, 0)),
            pl.BlockSpec((None, kw, tn), lambda i, j: (layer, 0, j)),
            pl.BlockSpec((None, kw, tn), lambda i, j: (layer, 1, j)),
            pl.BlockSpec((tm, tn), lambda i, j: (i, j)),
        ],
        out_specs=pl.BlockSpec((tm, tn), lambda i, j: (i, j)),
        out_shape=jax.ShapeDtypeStruct((r, d), F32),
        compiler_params=_params(("arbitrary", "arbitrary")),
        name="outproj",
    )(y_conv, y_delta, w_out, w_out, h)


def _first_argmax(vals, iota, n):
    m = jnp.max(vals, axis=-1, keepdims=True)
    idx = jnp.min(jnp.where(vals == m, iota, n), axis=-1, keepdims=True)
    return m, idx


def _router_kernel(h_ref, g_ref, w_ref, b_ref, u_ref, route_ref, cnt_ref, carry_sc):
    @pl.when(pl.program_id(0) == 0)
    def _():
        carry_sc[...] = jnp.zeros(carry_sc.shape, F32)

    x = h_ref[...]
    ms = jnp.mean(x * x, axis=-1, keepdims=True)
    u = x * lax.rsqrt(ms + EPS) * g_ref[...]
    u_ref[...] = u
    w = w_ref[...]
    uh = u.astype(BF16)
    ul = (u - uh.astype(F32)).astype(BF16)
    wh = w.astype(BF16)
    wl = (w - wh.astype(F32)).astype(BF16)
    logits = _dot(uh, wh) + (_dot(uh, wl) + _dot(ul, wh)) + b_ref[...]
    tm = x.shape[0]
    glog = logits[:, 0:N_GROUPS]
    elog = logits[:, N_GROUPS:N_GROUPS + N_EXPERTS]
    gi = lax.broadcasted_iota(jnp.int32, (tm, N_GROUPS), 1)
    gmax, gsel = _first_argmax(glog, gi, N_GROUPS)
    p_group = 1.0 / jnp.sum(jnp.exp(glog - gmax), axis=-1, keepdims=True)
    ei = lax.broadcasted_iota(jnp.int32, (tm, N_EXPERTS), 1)
    in_group = (ei >= gsel * EXPERTS_PER_GROUP) & (ei < (gsel + 1) * EXPERTS_PER_GROUP)
    neg = jnp.float32(-jnp.inf)
    cand = jnp.where(in_group, elog, neg)
    m1, i1 = _first_argmax(cand, ei, N_EXPERTS)
    cand2 = jnp.where(ei == i1, neg, cand)
    m2, i2 = _first_argmax(cand2, ei, N_EXPERTS)
    e2 = jnp.exp(m2 - m1)
    w1 = p_group / (1.0 + e2)
    w2 = p_group * e2 / (1.0 + e2)
    oh1 = (ei == i1).astype(F32)
    oh2 = (ei == i2).astype(F32)
    oh = oh1 + oh2
    ri = lax.broadcasted_iota(jnp.int32, (tm, tm), 0)
    ci = lax.broadcasted_iota(jnp.int32, (tm, tm), 1)
    before = _dot((ri > ci).astype(BF16), oh.astype(BF16)) + carry_sc[...]
    r1 = jnp.sum(before * oh1, axis=-1, keepdims=True)
    r2 = jnp.sum(before * oh2, axis=-1, keepdims=True)
    carry_sc[...] += jnp.sum(oh, axis=0, keepdims=True)
    cnt_ref[...] = carry_sc[...]
    li = lax.broadcasted_iota(jnp.int32, (tm, ROUTE_COLS), 1)
    rec = jnp.zeros((tm, ROUTE_COLS), F32)
    for k, col in enumerate((i1.astype(F32), i2.astype(F32), r1, r2, w1, w2)):
        rec = jnp.where(li == k, col, rec)
    route_ref[...] = rec


def _router(h, norm_g, w_group, b_group, w_router, b_router):
    r, d = h.shape
    tm = _divisor_tile(r, ROUTER_TM, 16)
    w = jnp.concatenate([w_group, w_router], axis=1)
    b = jnp.concatenate([b_group, b_router]).reshape(1, -1)
    nl = w.shape[1]
    return pl.pallas_call(
        _router_kernel,
        grid=(r // tm,),
        in_specs=[
            pl.BlockSpec((tm, d), lambda i: (i, 0)),
            pl.BlockSpec((1, d), lambda i: (0, 0)),
            pl.BlockSpec((d, nl), lambda i: (0, 0)),
            pl.BlockSpec((1, nl), lambda i: (0, 0)),
        ],
        out_specs=[
            pl.BlockSpec((tm, d), lambda i: (i, 0)),
            pl.BlockSpec((tm, ROUTE_COLS), lambda i: (i, 0)),
            pl.BlockSpec((1, N_EXPERTS), lambda i: (0, 0)),
        ],
        out_shape=[
            jax.ShapeDtypeStruct((r, d), F32),
            jax.ShapeDtypeStruct((r, ROUTE_COLS), F32),
            jax.ShapeDtypeStruct((1, N_EXPERTS), F32),
        ],
        scratch_shapes=[pltpu.VMEM((1, N_EXPERTS), F32)],
        compiler_params=_params(("arbitrary",)),
        name="router",
    )(h, norm_g.reshape(1, d), w, b)


def _dispatch_plan(route, cnt, tm, n_tiles):
    cnt = cnt[0].astype(jnp.int32)
    padded = ((cnt + tm - 1) // tm) * tm
    ends = jnp.cumsum(padded)
    off = ends - padded
    onehot = (route[:, 0:TOP_K, None] == jnp.arange(N_EXPERTS, dtype=F32)).astype(F32)
    pos = jnp.einsum("rke,e->rk", onehot, off.astype(F32), precision=HI) + route[:, TOP_K:2 * TOP_K]
    pos = pos.astype(jnp.int32)
    pos0, pos1 = pos[:, 0], pos[:, 1]
    n_used = ends[-1] // tm
    tiles = jnp.arange(n_tiles, dtype=jnp.int32)
    tile_e = jnp.sum((tiles[:, None] * tm >= ends[None, :]).astype(jnp.int32), axis=1)
    tile_e = jnp.minimum(jnp.where(tiles < n_used, tile_e, tile_e[n_used - 1]), N_EXPERTS - 1)
    return pos0, pos1, tile_e, n_used.reshape(1)


def _dispatch_kernel(pos0_ref, pos1_ref, u_ref, xs_init_ref, xs_ref, sem, *, t_rows):
    del xs_init_ref
    base = pl.program_id(0) * t_rows

    def body(r, c):
        src = u_ref.at[pl.ds(r, 1)]
        pltpu.make_async_copy(src, xs_ref.at[pl.ds(pos0_ref[base + r], 1)], sem).start()
        pltpu.make_async_copy(src, xs_ref.at[pl.ds(pos1_ref[base + r], 1)], sem).start()
        return c

    lax.fori_loop(0, t_rows, body, 0, unroll=DMA_UNROLL)
    for _ in range(2):
        pltpu.make_async_copy(u_ref, xs_ref.at[pl.ds(0, t_rows)], sem).wait()


def _dispatch(u, pos0, pos1, n_slots):
    r, d = u.shape
    t_rows = _divisor_tile(r, DISPATCH_T, SUBLANES)
    kern = functools.partial(_dispatch_kernel, t_rows=t_rows)
    return pl.pallas_call(
        kern,
        grid_spec=pltpu.PrefetchScalarGridSpec(
            num_scalar_prefetch=2,
            grid=(r // t_rows,),
            in_specs=[
                pl.BlockSpec((t_rows, d), lambda i, p0, p1: (i, 0)),
                pl.BlockSpec(memory_space=pl.ANY),
            ],
            out_specs=pl.BlockSpec(memory_space=pl.ANY),
            scratch_shapes=[pltpu.SemaphoreType.DMA(())],
        ),
        out_shape=jax.ShapeDtypeStruct((n_slots, d), F32),
        input_output_aliases={3: 0},
        compiler_params=_params(("arbitrary",)),
        name="moe_dispatch",
    )(pos0, pos1, u, jnp.zeros((n_slots, d), F32))


def _expert_kernel(te_ref, nu_ref, x_ref, wg_ref, wu_ref, wd_ref, y_ref, wg_b, wu_b, wd_b):
    i = pl.program_id(0)
    new_expert = (i == 0) | (te_ref[i] != te_ref[jnp.maximum(i - 1, 0)])

    @pl.when(new_expert)
    def _():
        wg_b[...] = wg_ref[...].astype(BF16)
        wu_b[...] = wu_ref[...].astype(BF16)
        wd_b[...] = wd_ref[...].astype(BF16)

    @pl.when(i < nu_ref[0])
    def _():
        x = x_ref[...].astype(BF16)
        hg = _dot(x, wg_b[...])
        hu = _dot(x, wu_b[...])
        hid = (_silu(hg) * hu).astype(BF16)
        y_ref[...] = _dot(hid, wd_b[...])


def _experts(xs, tile_e, n_used, w_gate, w_up, w_down, layer, tm):
    n_slots, d = xs.shape
    f = w_gate.shape[3]
    n_tiles = n_slots // tm
    row = lambda i, te, nu: (jnp.minimum(i, nu[0] - 1), 0)
    wmap = lambda i, te, nu: (layer, te[i], 0, 0)
    return pl.pallas_call(
        _expert_kernel,
        grid_spec=pltpu.PrefetchScalarGridSpec(
            num_scalar_prefetch=2,
            grid=(n_tiles,),
            in_specs=[
                pl.BlockSpec((tm, d), row),
                pl.BlockSpec((None, None, d, f), wmap),
                pl.BlockSpec((None, None, d, f), wmap),
                pl.BlockSpec((None, None, f, d), wmap),
            ],
            out_specs=pl.BlockSpec((tm, d), row),
            scratch_shapes=[pltpu.VMEM((d, f), BF16), pltpu.VMEM((d, f), BF16), pltpu.VMEM((f, d), BF16)],
        ),
        out_shape=jax.ShapeDtypeStruct((n_slots, d), F32),
        input_output_aliases={2: 0},
        compiler_params=_params(("arbitrary",)),
        name="moe_experts",
    )(tile_e, n_used, xs, w_gate, w_up, w_down)


def _combine_kernel(pos0_ref, pos1_ref, h_ref, route_ref, ys_ref, o_ref, buf0, buf1, sem, *, t_rows):
    base = pl.program_id(0) * t_rows

    def body(r, c):
        pltpu.make_async_copy(ys_ref.at[pl.ds(pos0_ref[base + r], 1)], buf0.at[pl.ds(r, 1)], sem).start()
        pltpu.make_async_copy(ys_ref.at[pl.ds(pos1_ref[base + r], 1)], buf1.at[pl.ds(r, 1)], sem).start()
        return c

    lax.fori_loop(0, t_rows, body, 0, unroll=DMA_UNROLL)
    for buf in (buf0, buf1):
        pltpu.make_async_copy(ys_ref.at[pl.ds(0, t_rows)], buf, sem).wait()
    g = route_ref[...]
    o_ref[...] = h_ref[...] + g[:, 4:5] * buf0[...] + g[:, 5:6] * buf1[...]


def _combine(h, route, ys, pos0, pos1):
    r, d = h.shape
    t_rows = _divisor_tile(r, COMBINE_T, SUBLANES)
    kern = functools.partial(_combine_kernel, t_rows=t_rows)
    return pl.pallas_call(
        kern,
        grid_spec=pltpu.PrefetchScalarGridSpec(
            num_scalar_prefetch=2,
            grid=(r // t_rows,),
            in_specs=[
                pl.BlockSpec((t_rows, d), lambda i, p0, p1: (i, 0)),
                pl.BlockSpec((t_rows, ROUTE_COLS), lambda i, p0, p1: (i, 0)),
                pl.BlockSpec(memory_space=pl.ANY),
            ],
            out_specs=pl.BlockSpec((t_rows, d), lambda i, p0, p1: (i, 0)),
            scratch_shapes=[pltpu.VMEM((t_rows, d), F32), pltpu.VMEM((t_rows, d), F32), pltpu.SemaphoreType.DMA(())],
        ),
        out_shape=jax.ShapeDtypeStruct((r, d), F32),
        compiler_params=_params(("arbitrary",)),
        name="moe_combine",
    )(pos0, pos1, h, route, ys)


def _hier_moe(h, norm_g, w_group, b_group, w_router, b_router, w_gate, w_up, w_down, layer):
    r = h.shape[0]
    tm = EXPERT_TM
    n_tiles = -(-(TOP_K * r + N_EXPERTS * (tm - 1)) // tm)
    u, route, cnt = _router(h, norm_g, w_group, b_group, w_router, b_router)
    pos0, pos1, tile_e, n_used = _dispatch_plan(route, cnt, tm, n_tiles)
    xs = _dispatch(u, pos0, pos1, n_tiles * tm)
    ys = _experts(xs, tile_e, n_used, w_gate, w_up, w_down, layer, tm)
    return _combine(h, route, ys, pos0, pos1)


def _final_kernel(h_ref, g_ref, o_ref):
    x = h_ref[...]
    ms = jnp.mean(x * x, axis=-1, keepdims=True)
    o_ref[0] = x * lax.rsqrt(ms + EPS) * g_ref[...]


def _final_norm(h, norm_g, nb, lp, seq, skip):
    d = h.shape[1]
    t_rows = _divisor_tile(seq, FINAL_T, SUBLANES)
    assert skip % SUBLANES == 0 and lp % SUBLANES == 0
    first_row = lambda b, t: (pl.multiple_of(b * lp + skip + t * t_rows, SUBLANES), 0)
    return pl.pallas_call(
        _final_kernel,
        grid=(nb, seq // t_rows),
        in_specs=[
            pl.BlockSpec((pl.Element(t_rows), pl.Element(d)), first_row),
            pl.BlockSpec((1, d), lambda b, t: (0, 0)),
        ],
        out_specs=pl.BlockSpec((1, t_rows, d), lambda b, t: (b, t, 0)),
        out_shape=jax.ShapeDtypeStruct((nb, seq, d), F32),
        compiler_params=_params(("arbitrary", "arbitrary")),
        name="final_norm",
    )(h, norm_g.reshape(1, d))


def kernel(x, meta, attn_norm, w_in, conv_dw_w, conv_dw_b, conv_ln_g, conv_ln_b, short_conv_w, a_log, dt_bias,
           delta_norm_g, w_out, ffn_norm, w_group, b_group, w_router, b_router, w_gate, w_up, w_down, final_norm):
    nb, seq, d = x.shape
    depth = w_in.shape[0]
    conv_w = conv_dw_w.shape[2]
    delta_w = short_conv_w.shape[2] // 3
    nh = delta_w // HEAD_DIM
    n_main = 2 * conv_w + 4 * delta_w
    assert w_in.shape[2] == n_main + 2 * nh and conv_w == delta_w
    ln = N_META + seq
    pad = (-ln) % CHUNK
    lp = ln + pad
    skip = pad + N_META

    meta_b = jnp.broadcast_to(meta[None].astype(x.dtype), (nb, N_META, d))
    h = jnp.concatenate([jnp.zeros((nb, pad, d), x.dtype), meta_b, x], axis=1).reshape(nb * lp, d)

    for l in range(depth):
        p, bd_col, bd_row = _inproj(h, attn_norm[l], w_in, l, n_main, lp, pad, nb)
        bd_row3 = bd_row.reshape(2 * nh, nb * lp // CHUNK, CHUNK).transpose(1, 0, 2)
        y_conv = _conformer_conv(p, conv_dw_w[l], conv_dw_b[l], conv_ln_g[l], conv_ln_b[l], nb, lp, conv_w)
        y_delta = _gated_deltanet(p, bd_col, bd_row3, short_conv_w[l], a_log[l], dt_bias[l], delta_norm_g[l],
                                  nb, lp, pad, delta_w, 2 * conv_w)
        h = _outproj(y_conv, y_delta, w_out, l, h)
        h = _hier_moe(h, ffn_norm[l], w_group[l], b_group[l], w_router[l], b_router[l], w_gate, w_up, w_down, l)
    return _final_norm(h, final_norm, nb, lp, seq, skip)
```

```python
import functools

import jax
import jax.numpy as jnp
from jax import lax
from jax.experimental import pallas as pl
from jax.experimental.pallas import tpu as pltpu

F32 = jnp.float32
BF16 = jnp.bfloat16
HI = lax.Precision.HIGHEST

EPS = 1e-6
CHUNK = 64
N_META = 16
CONV_GROUPS = 8
CONV_KERNEL = 31
HEAD_DIM = 128
SHORT_CONV = 4
N_GROUPS = 4
EXPERTS_PER_GROUP = 8
N_EXPERTS = N_GROUPS * EXPERTS_PER_GROUP
TOP_K = 2
ROUTE_COLS = 8
LANES = 128
SUBLANES = 8
CONV_HIST = 32
SC_HIST = 8
VMEM_LIMIT = 56 * 1024 * 1024

INPROJ_TM, INPROJ_TN = 1664, 512
CONV_CHUNKS = 13
DELTA_CHUNKS = 5
OUTPROJ_TM, OUTPROJ_TN = 1664, 512
ROUTER_TM = 640
EXPERT_TM = 256
DISPATCH_T = 640
COMBINE_T = 640
DMA_UNROLL = 8
FINAL_T = 512


def _divisor_tile(n, cap, mult):
    best = None
    for t in range(mult, min(n, cap) + 1, mult):
        if n % t == 0:
            best = t
    if best is None:
        raise ValueError(f"no tile for n={n} cap={cap} mult={mult}")
    return best


def _params(sem):
    return pltpu.CompilerParams(dimension_semantics=sem, vmem_limit_bytes=VMEM_LIMIT)


def _dot(a, b, precision=None):
    return jnp.dot(a, b, preferred_element_type=F32, precision=precision)


def _dot_nt(a, b, precision=None):
    return lax.dot_general(a, b, (((1,), (1,)), ((), ())), preferred_element_type=F32, precision=precision)


def _silu(x):
    return x * jax.nn.sigmoid(x)


def _softplus(x):
    return jnp.maximum(x, 0.0) + jnp.log1p(jnp.exp(-jnp.abs(x)))


def _dot_split(a, b, n_parts, split_lhs):
    x = a if split_lhs else b
    acc = None
    for _ in range(n_parts):
        piece = x.astype(BF16)
        term = _dot(piece, b) if split_lhs else _dot(a, piece)
        acc = term if acc is None else acc + term
        x = x - piece.astype(F32)
    return acc


def _pack_bf16_pairs(x):
    n = x.shape[1] // 2
    lo = lax.bitcast_convert_type(x[:, 0:n].astype(F32), jnp.uint32)
    hi = lax.bitcast_convert_type(x[:, n:2 * n].astype(F32), jnp.uint32)
    return (lo >> 16) | hi


def _unpack_bf16_pairs(w):
    lo = lax.bitcast_convert_type(w << 16, F32).astype(BF16)
    hi = lax.bitcast_convert_type(w & jnp.uint32(0xFFFF0000), F32).astype(BF16)
    return lo, hi


def _causal_taps(win, tap_w, n_taps, first_tap):
    acc = jnp.zeros((CHUNK,) + win.shape[1:], F32)
    for res in range(SUBLANES):
        offs = [o for o in range(first_tap, first_tap + n_taps) if o % SUBLANES == res]
        if not offs:
            continue
        shifted = win[res:, :] if res else win
        for o in offs:
            a = o - res
            acc = acc + tap_w(o - first_tap) * shifted[a:a + CHUNK, :]
    return acc


def _inproj_kernel(h_ref, g_ref, w_ref, wbd_ref, p_ref, bdc_ref, bdr_ref, u_sc, *, tm, rb, n_bd, lp, pad, nb):
    i = pl.program_id(0)
    j = pl.program_id(1)

    @pl.when(j == 0)
    def _():
        wbd = wbd_ref[:, 0:n_bd].astype(BF16)
        for blk in range(tm // rb):
            rows = slice(blk * rb, (blk + 1) * rb)
            x = h_ref[rows, :]
            ms = jnp.mean(x * x, axis=-1, keepdims=True)
            u = x * lax.rsqrt(ms + EPS) * g_ref[...]
            row = i * tm + blk * rb + lax.broadcasted_iota(jnp.int32, (rb, 1), 0)
            valid = (row >= pad) & (row < lp)
            for b in range(1, nb):
                valid = valid | ((row >= b * lp + pad) & (row < (b + 1) * lp))
            ub = jnp.where(valid, u, 0.0).astype(BF16)
            u_sc[rows, :] = ub
            bdc_ref[rows, :] = _dot(ub, wbd)
            bdr_ref[:, rows] = lax.dot_general(wbd, ub, (((0,), (1,)), ((), ())), preferred_element_type=F32)

    p_ref[...] = _dot(u_sc[...], w_ref[...].astype(BF16))


def _inproj(h, norm_g, w_in, layer, n_main, lp, pad, nb):
    r, d = h.shape
    n_bd = w_in.shape[2] - n_main
    tm = _divisor_tile(r, INPROJ_TM, LANES)
    tn = _divisor_tile(n_main, INPROJ_TN, LANES)
    assert n_main % LANES == 0 and n_bd <= LANES
    kern = functools.partial(_inproj_kernel, tm=tm, rb=LANES, n_bd=n_bd, lp=lp, pad=pad, nb=nb)
    return pl.pallas_call(
        kern,
        grid=(r // tm, n_main // tn),
        in_specs=[
            pl.BlockSpec((tm, d), lambda i, j: (i, 0)),
            pl.BlockSpec((1, d), lambda i, j: (0, 0)),
            pl.BlockSpec((None, d, tn), lambda i, j: (layer, 0, j)),
            pl.BlockSpec((None, d, LANES), lambda i, j: (layer, 0, n_main // LANES)),
        ],
        out_specs=[
            pl.BlockSpec((tm, tn), lambda i, j: (i, j)),
            pl.BlockSpec((tm, n_bd), lambda i, j: (i, 0)),
            pl.BlockSpec((n_bd, tm), lambda i, j: (0, i)),
        ],
        out_shape=[
            jax.ShapeDtypeStruct((r, n_main), F32),
            jax.ShapeDtypeStruct((r, n_bd), F32),
            jax.ShapeDtypeStruct((n_bd, r), F32),
        ],
        scratch_shapes=[pltpu.VMEM((tm, d), BF16)],
        compiler_params=_params(("arbitrary", "arbitrary")),
        name="inproj",
    )(h, norm_g.reshape(1, d), w_in, w_in)


def _conv_kernel(a_ref, b_ref, w_ref, bias_ref, lg_ref, lb_ref, o_ref, ybuf, *, t_rows, width):
    t = pl.program_id(1)
    n_blk = t_rows // CHUNK

    @pl.when(t == 0)
    def _():
        ybuf[0:CONV_HIST, :] = jnp.zeros((CONV_HIST, width), F32)

    @pl.when(t > 0)
    def _():
        ybuf[0:CONV_HIST, :] = ybuf[t_rows:t_rows + CONV_HIST, :]

    def glu_body(r, c):
        r0 = pl.multiple_of(r * CHUNK, CHUNK)
        a = a_ref[pl.ds(r0, CHUNK), :]
        g = b_ref[pl.ds(r0, CHUNK), :]
        ybuf[pl.ds(CONV_HIST + r0, CHUNK), :] = a * jax.nn.sigmoid(g)
        return c

    lax.fori_loop(0, n_blk, glu_body, 0)

    first_tap = CONV_HIST - (CONV_KERNEL - 1)

    def body(r, c):
        r0 = pl.multiple_of(r * CHUNK, CHUNK)
        for gi in range(width // LANES):
            ls = slice(gi * LANES, (gi + 1) * LANES)
            win = ybuf[pl.ds(r0, CHUNK + CONV_HIST), ls]
            acc = _causal_taps(win, lambda k: w_ref[k:k + 1, ls], CONV_KERNEL, first_tap)
            y = acc + bias_ref[:, ls]
            mu = jnp.mean(y, axis=-1, keepdims=True)
            dlt = y - mu
            var = jnp.mean(dlt * dlt, axis=-1, keepdims=True)
            yn = dlt * lax.rsqrt(var + EPS) * lg_ref[:, ls] + lb_ref[:, ls]
            o_ref[pl.ds(r0, CHUNK), ls] = _silu(yn).astype(BF16)
        return c

    lax.fori_loop(0, n_blk, body, 0)


def _conformer_conv(p, w_dw, b_dw, ln_g, ln_b, nb, lp, width):
    r = p.shape[0]
    nch = lp // CHUNK
    t_rows = CHUNK * _divisor_tile(nch, CONV_CHUNKS, 1)
    nt = lp // t_rows
    assert width // LANES == CONV_GROUPS
    kern = functools.partial(_conv_kernel, t_rows=t_rows, width=width)
    vec = lambda b, t: (0, 0)
    return pl.pallas_call(
        kern,
        grid=(nb, nt),
        in_specs=[
            pl.BlockSpec((t_rows, width), lambda b, t: (b * nt + t, 0)),
            pl.BlockSpec((t_rows, width), lambda b, t: (b * nt + t, 1)),
            pl.BlockSpec((CONV_KERNEL, width), vec),
            pl.BlockSpec((1, width), vec),
            pl.BlockSpec((1, width), vec),
            pl.BlockSpec((1, width), vec),
        ],
        out_specs=pl.BlockSpec((t_rows, width), lambda b, t: (b * nt + t, 0)),
        out_shape=jax.ShapeDtypeStruct((r, width), BF16),
        scratch_shapes=[pltpu.VMEM((t_rows + CONV_HIST, width), F32)],
        compiler_params=_params(("arbitrary", "arbitrary")),
        name="conformer_conv",
    )(p, p, w_dw, b_dw.reshape(1, width), ln_g.reshape(1, width), ln_b.reshape(1, width))


def _delta_kernel(q_ref, k_ref, v_ref, z_ref, bdc_ref, bdr_ref, scw_ref, alr_ref, dtr_ref, alc_ref, dtc_ref,
                  ng_ref, o_ref, qbuf, kbuf, vbuf, s_sc, *, t_rows, pad, width, nh):
    t = pl.program_id(1)
    n_blk = t_rows // CHUNK
    bufs = (qbuf, kbuf, vbuf)
    srcs = (q_ref, k_ref, v_ref)

    @pl.when(t == 0)
    def _():
        s_sc[...] = jnp.zeros(s_sc.shape, F32)
        for buf in bufs:
            buf[0:SC_HIST, :] = jnp.zeros((SC_HIST, width), F32)

    @pl.when(t > 0)
    def _():
        for buf in bufs:
            buf[0:SC_HIST, :] = buf[t_rows:t_rows + SC_HIST, :]

    def copy_body(r, c):
        r0 = pl.multiple_of(r * CHUNK, CHUNK)
        for buf, src in zip(bufs, srcs):
            buf[pl.ds(SC_HIST + r0, CHUNK), :] = src[pl.ds(r0, CHUNK), :]
        return c

    lax.fori_loop(0, n_blk, copy_body, 0)

    ri = lax.broadcasted_iota(jnp.int32, (CHUNK, CHUNK), 0)
    ci = lax.broadcasted_iota(jnp.int32, (CHUNK, CHUNK), 1)
    incl = ri >= ci
    strict = ri > ci
    tril = incl.astype(BF16)
    triu = (ri <= ci).astype(BF16)
    eye = (ri == ci).astype(F32)

    def head_expand(lanes):
        eh = lax.broadcasted_iota(jnp.int32, (nh, nh * lanes), 0)
        ec = lax.broadcasted_iota(jnp.int32, (nh, nh * lanes), 1)
        return ((ec >= eh * lanes) & (ec < (eh + 1) * lanes)).astype(BF16)

    expand_c = head_expand(CHUNK)
    expand_d = head_expand(HEAD_DIM)
    first_tap = SC_HIST - (SHORT_CONV - 1)
    q_scale = HEAD_DIM ** -0.5
    heads = range(nh)

    def short_conv(buf, part, r0):
        win = buf[pl.ds(r0, CHUNK + SC_HIST), :]
        acc = _causal_taps(win, lambda k: scw_ref[k:k + 1, part * width:(part + 1) * width], SHORT_CONV, first_tap)
        return _silu(acc)

    def body(c, carry):
        r0 = pl.multiple_of(c * CHUNK, CHUNK)
        lrow = t * t_rows + r0 + lax.broadcasted_iota(jnp.int32, (CHUNK, 1), 0)
        lcol = t * t_rows + r0 + lax.broadcasted_iota(jnp.int32, (1, CHUNK), 1)
        bl = bdc_ref[pl.ds(r0, CHUNK), :]
        br = bdr_ref[c]
        beta_col = jnp.where(lrow >= pad, jax.nn.sigmoid(bl[:, 0:nh]), 0.0)
        beta_row = jnp.where(lcol >= pad, jax.nn.sigmoid(br[0:nh, :]), 0.0)
        g_col = jnp.where(lrow >= pad, -jnp.exp(alr_ref[...]) * _softplus(bl[:, nh:2 * nh] + dtr_ref[...]), 0.0)
        g_row = jnp.where(lcol >= pad, -jnp.exp(alc_ref[...]) * _softplus(br[nh:2 * nh, :] + dtc_ref[...]), 0.0)
        gc_col = _dot_split(tril, g_col, 3, split_lhs=False)
        gc_row = _dot_split(g_row, triu, 3, split_lhs=True)
        eg_row = jnp.exp(gc_row)
        ekd_row = jnp.exp(gc_row[:, CHUNK - 1:CHUNK] - gc_row)
        gc_x = _dot_split(gc_col, expand_c, 3, split_lhs=True)
        beta_x = _dot_split(beta_col, expand_c, 2, split_lhs=True)
        eg_x = _dot_split(jnp.exp(gc_col), expand_d, 2, split_lhs=True)

        qc = short_conv(qbuf, 0, r0)
        kc = short_conv(kbuf, 1, r0)
        vc = short_conv(vbuf, 2, r0)
        zc = z_ref[pl.ds(r0, CHUNK), :]

        hd = [slice(h * HEAD_DIM, (h + 1) * HEAD_DIM) for h in heads]
        hc = [slice(h * CHUNK, (h + 1) * CHUNK) for h in heads]
        qn = [qc[:, s] * lax.rsqrt(jnp.sum(qc[:, s] * qc[:, s], axis=-1, keepdims=True) + EPS) * q_scale for s in hd]
        kn = [kc[:, s] * lax.rsqrt(jnp.sum(kc[:, s] * kc[:, s], axis=-1, keepdims=True) + EPS) for s in hd]
        kb16 = [k.astype(BF16) for k in kn]
        v16 = [vc[:, s].astype(BF16) for s in hd]
        decay = [jnp.where(incl, jnp.exp(jnp.where(incl, gc_x[:, hc[h]] - gc_row[h:h + 1, :], 0.0)), 0.0) for h in heads]
        kq = [_dot_nt(jnp.concatenate([kb16[h], qn[h].astype(BF16)], axis=0), kb16[h]) for h in heads]
        xp = [jnp.where(strict, -(kq[h][0:CHUNK] * beta_x[:, hc[h]] * decay[h]), 0.0) for h in heads]
        intra = [kq[h][CHUNK:2 * CHUNK] * decay[h] for h in heads]
        ainv = [eye + x for x in xp]
        n_sq = 1
        while 2 * n_sq < CHUNK:
            xp16 = [x.astype(BF16) for x in xp]
            xp = [_dot(x, x) for x in xp16]
            ainv = [a + _dot(a.astype(BF16), x.astype(BF16)) for a, x in zip(ainv, xp)]
            n_sq *= 2
        u = [_dot((ainv[h] * beta_row[h:h + 1, :]).astype(BF16), v16[h]) for h in heads]
        w = [_dot((ainv[h] * (beta_row[h:h + 1, :] * eg_row[h:h + 1, :])).astype(BF16), kb16[h]) for h in heads]
        q_dec = [qn[h] * eg_x[:, hd[h]] for h in heads]
        s_old = [s_sc[h] for h in heads]
        wq_s = [_dot(jnp.concatenate([w[h], q_dec[h]], axis=0).astype(BF16), s_old[h].astype(BF16)) for h in heads]
        v_new = [u[h] - wq_s[h][0:CHUNK] for h in heads]
        kd_t = [kn[h].T * ekd_row[h:h + 1, :] for h in heads]
        iv = [_dot(jnp.concatenate([intra[h], kd_t[h]], axis=0).astype(BF16), v_new[h].astype(BF16)) for h in heads]
        for h in heads:
            o = wq_s[h][CHUNK:2 * CHUNK] + iv[h][0:CHUNK]
            s_sc[h] = s_old[h] * eg_x[CHUNK - 1:CHUNK, hd[h]] + iv[h][CHUNK:CHUNK + HEAD_DIM]
            on = o * lax.rsqrt(jnp.mean(o * o, axis=-1, keepdims=True) + EPS) * ng_ref[...]
            o_ref[pl.ds(r0, CHUNK), hd[h]] = (on * _silu(zc[:, hd[h]])).astype(BF16)
        return carry

    lax.fori_loop(0, n_blk, body, 0)


def _gated_deltanet(p, bd_col, bd_row3, sc_w, a_log, dt_bias, norm_g, nb, lp, pad, width, col0):
    r = p.shape[0]
    nh = width // HEAD_DIM
    nch = lp // CHUNK
    cpt = _divisor_tile(nch, DELTA_CHUNKS, 1)
    t_rows = CHUNK * cpt
    nt = lp // t_rows
    cb = col0 // width
    assert col0 % width == 0
    kern = functools.partial(_delta_kernel, t_rows=t_rows, pad=pad, width=width, nh=nh)
    vec = lambda b, t: (0, 0)
    part = lambda off: pl.BlockSpec((t_rows, width), lambda b, t: (b * nt + t, cb + off))
    return pl.pallas_call(
        kern,
        grid=(nb, nt),
        in_specs=[
            part(0), part(1), part(2), part(3),
            pl.BlockSpec((t_rows, 2 * nh), lambda b, t: (b * nt + t, 0)),
            pl.BlockSpec((cpt, 2 * nh, CHUNK), lambda b, t: (b * nt + t, 0, 0)),
            pl.BlockSpec((SHORT_CONV, 3 * width), vec),
            pl.BlockSpec((1, nh), vec),
            pl.BlockSpec((1, nh), vec),
            pl.BlockSpec((nh, 1), vec),
            pl.BlockSpec((nh, 1), vec),
            pl.BlockSpec((1, HEAD_DIM), vec),
        ],
        out_specs=pl.BlockSpec((t_rows, width), lambda b, t: (b * nt + t, 0)),
        out_shape=jax.ShapeDtypeStruct((r, width), BF16),
        scratch_shapes=[
            pltpu.VMEM((t_rows + SC_HIST, width), F32),
            pltpu.VMEM((t_rows + SC_HIST, width), F32),
            pltpu.VMEM((t_rows + SC_HIST, width), F32),
            pltpu.VMEM((nh, HEAD_DIM, HEAD_DIM), F32),
        ],
        compiler_params=_params(("arbitrary", "arbitrary")),
        name="gated_deltanet",
    )(p, p, p, p, bd_col, bd_row3, sc_w, a_log.reshape(1, nh), dt_bias.reshape(1, nh),
      a_log.reshape(nh, 1), dt_bias.reshape(nh, 1), norm_g.reshape(1, HEAD_DIM))


def _outproj_kernel(yc_ref, yd_ref, w1_ref, w2_ref, h_ref, o_ref):
    acc = _dot(yc_ref[...], w1_ref[...].astype(BF16))
    acc = acc + _dot(yd_ref[...], w2_ref[...].astype(BF16))
    o_ref[...] = h_ref[...] + acc


def _outproj(y_conv, y_delta, w_out, layer, h):
    r, d = h.shape
    kw = y_conv.shape[1]
    tm = _divisor_tile(r, OUTPROJ_TM, 16)
    tn = _divisor_tile(d, OUTPROJ_TN, LANES)
    return pl.pallas_call(
        _outproj_kernel,
        grid=(r // tm, d // tn),
        in_specs=[
            pl.BlockSpec((tm, kw), lambda i, j: (i, 0)),
            pl.BlockSpec((tm, kw), lambda i, j: (i, 0)),
            pl.BlockSpec((None, kw, tn), lambda i, j: (layer, 0, j)),
            pl.BlockSpec((None, kw, tn), lambda i, j: (layer, 1, j)),
            pl.BlockSpec((tm, tn), lambda i, j: (i, j)),
        ],
        out_specs=pl.BlockSpec((tm, tn), lambda i, j: (i, j)),
        out_shape=jax.ShapeDtypeStruct((r, d), F32),
        compiler_params=_params(("arbitrary", "arbitrary")),
        name="outproj",
    )(y_conv, y_delta, w_out, w_out, h)


def _first_argmax(vals, iota, n):
    m = jnp.max(vals, axis=-1, keepdims=True)
    idx = jnp.min(jnp.where(vals == m, iota, n), axis=-1, keepdims=True)
    return m, idx


def _router_kernel(h_ref, g_ref, w_ref, b_ref, u_ref, route_ref, cnt_ref, carry_sc):
    @pl.when(pl.program_id(0) == 0)
    def _():
        carry_sc[...] = jnp.zeros(carry_sc.shape, F32)

    x = h_ref[...]
    ms = jnp.mean(x * x, axis=-1, keepdims=True)
    u = x * lax.rsqrt(ms + EPS) * g_ref[...]
    w = w_ref[...]
    uh = u.astype(BF16)
    ul = (u - uh.astype(F32)).astype(BF16)
    u_ref[...] = _pack_bf16_pairs(uh)
    wh = w.astype(BF16)
    wl = (w - wh.astype(F32)).astype(BF16)
    logits = _dot(uh, wh) + (_dot(uh, wl) + _dot(ul, wh)) + b_ref[...]
    tm = x.shape[0]
    glog = logits[:, 0:N_GROUPS]
    elog = logits[:, N_GROUPS:N_GROUPS + N_EXPERTS]
    gi = lax.broadcasted_iota(jnp.int32, (tm, N_GROUPS), 1)
    gmax, gsel = _first_argmax(glog, gi, N_GROUPS)
    p_group = 1.0 / jnp.sum(jnp.exp(glog - gmax), axis=-1, keepdims=True)
    ei = lax.broadcasted_iota(jnp.int32, (tm, N_EXPERTS), 1)
    in_group = (ei >= gsel * EXPERTS_PER_GROUP) & (ei < (gsel + 1) * EXPERTS_PER_GROUP)
    neg = jnp.float32(-jnp.inf)
    cand = jnp.where(in_group, elog, neg)
    m1, i1 = _first_argmax(cand, ei, N_EXPERTS)
    cand2 = jnp.where(ei == i1, neg, cand)
    m2, i2 = _first_argmax(cand2, ei, N_EXPERTS)
    e2 = jnp.exp(m2 - m1)
    w1 = p_group / (1.0 + e2)
    w2 = p_group * e2 / (1.0 + e2)
    oh1 = (ei == i1).astype(F32)
    oh2 = (ei == i2).astype(F32)
    oh = oh1 + oh2
    ri = lax.broadcasted_iota(jnp.int32, (tm, tm), 0)
    ci = lax.broadcasted_iota(jnp.int32, (tm, tm), 1)
    before = _dot((ri > ci).astype(BF16), oh.astype(BF16)) + carry_sc[...]
    r1 = jnp.sum(before * oh1, axis=-1, keepdims=True)
    r2 = jnp.sum(before * oh2, axis=-1, keepdims=True)
    carry_sc[...] += jnp.sum(oh, axis=0, keepdims=True)
    cnt_ref[...] = carry_sc[...]
    li = lax.broadcasted_iota(jnp.int32, (tm, ROUTE_COLS), 1)
    rec = jnp.zeros((tm, ROUTE_COLS), F32)
    for k, col in enumerate((i1.astype(F32), i2.astype(F32), r1, r2, w1, w2)):
        rec = jnp.where(li == k, col, rec)
    route_ref[...] = rec


def _router(h, norm_g, w_group, b_group, w_router, b_router):
    r, d = h.shape
    tm = _divisor_tile(r, ROUTER_TM, 16)
    w = jnp.concatenate([w_group, w_router], axis=1)
    b = jnp.concatenate([b_group, b_router]).reshape(1, -1)
    nl = w.shape[1]
    return pl.pallas_call(
        _router_kernel,
        grid=(r // tm,),
        in_specs=[
            pl.BlockSpec((tm, d), lambda i: (i, 0)),
            pl.BlockSpec((1, d), lambda i: (0, 0)),
            pl.BlockSpec((d, nl), lambda i: (0, 0)),
            pl.BlockSpec((1, nl), lambda i: (0, 0)),
        ],
        out_specs=[
            pl.BlockSpec((tm, d // 2), lambda i: (i, 0)),
            pl.BlockSpec((tm, ROUTE_COLS), lambda i: (i, 0)),
            pl.BlockSpec((1, N_EXPERTS), lambda i: (0, 0)),
        ],
        out_shape=[
            jax.ShapeDtypeStruct((r, d // 2), jnp.uint32),
            jax.ShapeDtypeStruct((r, ROUTE_COLS), F32),
            jax.ShapeDtypeStruct((1, N_EXPERTS), F32),
        ],
        scratch_shapes=[pltpu.VMEM((1, N_EXPERTS), F32)],
        compiler_params=_params(("arbitrary",)),
        name="router",
    )(h, norm_g.reshape(1, d), w, b)


def _dispatch_plan(route, cnt, tm, n_tiles):
    cnt = cnt[0].astype(jnp.int32)
    padded = ((cnt + tm - 1) // tm) * tm
    ends = jnp.cumsum(padded)
    off = ends - padded
    onehot = (route[:, 0:TOP_K, None] == jnp.arange(N_EXPERTS, dtype=F32)).astype(F32)
    pos = jnp.einsum("rke,e->rk", onehot, off.astype(F32), precision=HI) + route[:, TOP_K:2 * TOP_K]
    pos = pos.astype(jnp.int32)
    pos0, pos1 = pos[:, 0], pos[:, 1]
    n_used = ends[-1] // tm
    tiles = jnp.arange(n_tiles, dtype=jnp.int32)
    tile_e = jnp.sum((tiles[:, None] * tm >= ends[None, :]).astype(jnp.int32), axis=1)
    tile_e = jnp.minimum(jnp.where(tiles < n_used, tile_e, tile_e[n_used - 1]), N_EXPERTS - 1)
    return pos0, pos1, tile_e, n_used.reshape(1)


def _dispatch_kernel(pos0_ref, pos1_ref, u_ref, xs_init_ref, xs_ref, sem, *, t_rows):
    del xs_init_ref
    base = pl.program_id(0) * t_rows

    def body(r, c):
        src = u_ref.at[pl.ds(r, 1)]
        pltpu.make_async_copy(src, xs_ref.at[pl.ds(pos0_ref[base + r], 1)], sem).start()
        pltpu.make_async_copy(src, xs_ref.at[pl.ds(pos1_ref[base + r], 1)], sem).start()
        return c

    lax.fori_loop(0, t_rows, body, 0, unroll=DMA_UNROLL)
    for _ in range(2):
        pltpu.make_async_copy(u_ref, xs_ref.at[pl.ds(0, t_rows)], sem).wait()


def _dispatch(u, pos0, pos1, n_slots):
    r, d = u.shape
    t_rows = _divisor_tile(r, DISPATCH_T, SUBLANES)
    kern = functools.partial(_dispatch_kernel, t_rows=t_rows)
    return pl.pallas_call(
        kern,
        grid_spec=pltpu.PrefetchScalarGridSpec(
            num_scalar_prefetch=2,
            grid=(r // t_rows,),
            in_specs=[
                pl.BlockSpec((t_rows, d), lambda i, p0, p1: (i, 0)),
                pl.BlockSpec(memory_space=pl.ANY),
            ],
            out_specs=pl.BlockSpec(memory_space=pl.ANY),
            scratch_shapes=[pltpu.SemaphoreType.DMA(())],
        ),
        out_shape=jax.ShapeDtypeStruct((n_slots, d), u.dtype),
        input_output_aliases={3: 0},
        compiler_params=_params(("arbitrary",)),
        name="moe_dispatch",
    )(pos0, pos1, u, jnp.zeros((n_slots, d), u.dtype))


def _expert_kernel(te_ref, nu_ref, x_ref, wg_ref, wu_ref, wd_ref, y_ref, wg_b, wu_b, wd_b):
    i = pl.program_id(0)
    new_expert = (i == 0) | (te_ref[i] != te_ref[jnp.maximum(i - 1, 0)])

    @pl.when(new_expert)
    def _():
        wg_b[...] = wg_ref[...].astype(BF16)
        wu_b[...] = wu_ref[...].astype(BF16)
        wd_b[...] = wd_ref[...].astype(BF16)

    @pl.when(i < nu_ref[0])
    def _():
        xa, xb = _unpack_bf16_pairs(x_ref[...])
        half = xa.shape[1]
        hg = _dot(xa, wg_b[0:half, :]) + _dot(xb, wg_b[half:2 * half, :])
        hu = _dot(xa, wu_b[0:half, :]) + _dot(xb, wu_b[half:2 * half, :])
        hid = (_silu(hg) * hu).astype(BF16)
        y_ref[...] = _dot(hid, wd_b[...])

    @pl.when(i >= nu_ref[0])
    def _():
        y_ref[...] = jnp.zeros(y_ref.shape, F32)


def _experts(xs, tile_e, n_used, w_gate, w_up, w_down, layer, tm):
    n_slots = xs.shape[0]
    _, _, d, f = w_gate.shape
    assert xs.shape[1] * 2 == d
    n_tiles = n_slots // tm
    used_row = lambda i, te, nu: (jnp.minimum(i, nu[0] - 1), 0)
    wmap = lambda i, te, nu: (layer, te[i], 0, 0)
    return pl.pallas_call(
        _expert_kernel,
        grid_spec=pltpu.PrefetchScalarGridSpec(
            num_scalar_prefetch=2,
            grid=(n_tiles,),
            in_specs=[
                pl.BlockSpec((tm, d // 2), used_row),
                pl.BlockSpec((None, None, d, f), wmap),
                pl.BlockSpec((None, None, d, f), wmap),
                pl.BlockSpec((None, None, f, d), wmap),
            ],
            out_specs=pl.BlockSpec((tm, d), lambda i, te, nu: (i, 0)),
            scratch_shapes=[pltpu.VMEM((d, f), BF16), pltpu.VMEM((d, f), BF16), pltpu.VMEM((f, d), BF16)],
        ),
        out_shape=jax.ShapeDtypeStruct((n_slots, d), F32),
        compiler_params=_params(("arbitrary",)),
        name="moe_experts",
    )(tile_e, n_used, xs, w_gate, w_up, w_down)


def _combine_kernel(pos0_ref, pos1_ref, h_ref, route_ref, ys_ref, o_ref, buf0, buf1, sem, *, t_rows):
    base = pl.program_id(0) * t_rows

    def body(r, c):
        pltpu.make_async_copy(ys_ref.at[pl.ds(pos0_ref[base + r], 1)], buf0.at[pl.ds(r, 1)], sem).start()
        pltpu.make_async_copy(ys_ref.at[pl.ds(pos1_ref[base + r], 1)], buf1.at[pl.ds(r, 1)], sem).start()
        return c

    lax.fori_loop(0, t_rows, body, 0, unroll=DMA_UNROLL)
    for buf in (buf0, buf1):
        pltpu.make_async_copy(ys_ref.at[pl.ds(0, t_rows)], buf, sem).wait()
    g = route_ref[...]
    o_ref[...] = h_ref[...] + g[:, 4:5] * buf0[...] + g[:, 5:6] * buf1[...]


def _combine(h, route, ys, pos0, pos1):
    r, d = h.shape
    t_rows = _divisor_tile(r, COMBINE_T, SUBLANES)
    kern = functools.partial(_combine_kernel, t_rows=t_rows)
    return pl.pallas_call(
        kern,
        grid_spec=pltpu.PrefetchScalarGridSpec(
            num_scalar_prefetch=2,
            grid=(r // t_rows,),
            in_specs=[
                pl.BlockSpec((t_rows, d), lambda i, p0, p1: (i, 0)),
                pl.BlockSpec((t_rows, ROUTE_COLS), lambda i, p0, p1: (i, 0)),
                pl.BlockSpec(memory_space=pl.ANY),
            ],
            out_specs=pl.BlockSpec((t_rows, d), lambda i, p0, p1: (i, 0)),
            scratch_shapes=[pltpu.VMEM((t_rows, d), F32), pltpu.VMEM((t_rows, d), F32), pltpu.SemaphoreType.DMA(())],
        ),
        out_shape=jax.ShapeDtypeStruct((r, d), F32),
        compiler_params=_params(("arbitrary",)),
        name="moe_combine",
    )(pos0, pos1, h, route, ys)


def _hier_moe(h, norm_g, w_group, b_group, w_router, b_router, w_gate, w_up, w_down, layer):
    r = h.shape[0]
    tm = EXPERT_TM
    n_tiles = -(-(TOP_K * r + N_EXPERTS * (tm - 1)) // tm)
    u, route, cnt = _router(h, norm_g, w_group, b_group, w_router, b_router)
    pos0, pos1, tile_e, n_used = _dispatch_plan(route, cnt, tm, n_tiles)
    xs = _dispatch(u, pos0, pos1, n_tiles * tm)
    ys = _experts(xs, tile_e, n_used, w_gate, w_up, w_down, layer, tm)
    return _combine(h, route, ys, pos0, pos1)


def _final_kernel(h_ref, g_ref, o_ref):
    x = h_ref[...]
    ms = jnp.mean(x * x, axis=-1, keepdims=True)
    o_ref[0] = x * lax.rsqrt(ms + EPS) * g_ref[...]


def _final_norm(h, norm_g, nb, lp, seq, skip):
    d = h.shape[1]
    t_rows = _divisor_tile(seq, FINAL_T, SUBLANES)
    assert skip % SUBLANES == 0 and lp % SUBLANES == 0
    first_row = lambda b, t: (pl.multiple_of(b * lp + skip + t * t_rows, SUBLANES), 0)
    return pl.pallas_call(
        _final_kernel,
        grid=(nb, seq // t_rows),
        in_specs=[
            pl.BlockSpec((pl.Element(t_rows), pl.Element(d)), first_row),
            pl.BlockSpec((1, d), lambda b, t: (0, 0)),
        ],
        out_specs=pl.BlockSpec((1, t_rows, d), lambda b, t: (b, t, 0)),
        out_shape=jax.ShapeDtypeStruct((nb, seq, d), F32),
        compiler_params=_params(("arbitrary", "arbitrary")),
        name="final_norm",
    )(h, norm_g.reshape(1, d))


def kernel(x, meta, attn_norm, w_in, conv_dw_w, conv_dw_b, conv_ln_g, conv_ln_b, short_conv_w, a_log, dt_bias,
           delta_norm_g, w_out, ffn_norm, w_group, b_group, w_router, b_router, w_gate, w_up, w_down, final_norm):
    nb, seq, d = x.shape
    depth = w_in.shape[0]
    conv_w = conv_dw_w.shape[2]
    delta_w = short_conv_w.shape[2] // 3
    nh = delta_w // HEAD_DIM
    n_main = 2 * conv_w + 4 * delta_w
    assert w_in.shape[2] == n_main + 2 * nh and conv_w == delta_w
    ln = N_META + seq
    pad = (-ln) % CHUNK
    lp = ln + pad
    skip = pad + N_META

    meta_b = jnp.broadcast_to(meta[None].astype(x.dtype), (nb, N_META, d))
    h = jnp.concatenate([jnp.zeros((nb, pad, d), x.dtype), meta_b, x], axis=1).reshape(nb * lp, d)

    for l in range(depth):
        p, bd_col, bd_row = _inproj(h, attn_norm[l], w_in, l, n_main, lp, pad, nb)
        bd_row3 = bd_row.reshape(2 * nh, nb * lp // CHUNK, CHUNK).transpose(1, 0, 2)
        y_conv = _conformer_conv(p, conv_dw_w[l], conv_dw_b[l], conv_ln_g[l], conv_ln_b[l], nb, lp, conv_w)
        y_delta = _gated_deltanet(p, bd_col, bd_row3, short_conv_w[l], a_log[l], dt_bias[l], delta_norm_g[l],
                                  nb, lp, pad, delta_w, 2 * conv_w)
        h = _outproj(y_conv, y_delta, w_out, l, h)
        h = _hier_moe(h, ffn_norm[l], w_group[l], b_group[l], w_router[l], b_router[l], w_gate, w_up, w_down, l)
    return _final_norm(h, final_norm, nb, lp, seq, skip)
```

```python
import functools

import jax
import jax.numpy as jnp
from jax import lax
from jax.experimental import pallas as pl
from jax.experimental.pallas import tpu as pltpu

F32 = jnp.float32
BF16 = jnp.bfloat16
HI = lax.Precision.HIGHEST

EPS = 1e-6
CHUNK = 64
N_META = 16
CONV_GROUPS = 8
CONV_KERNEL = 31
HEAD_DIM = 128
SHORT_CONV = 4
N_GROUPS = 4
EXPERTS_PER_GROUP = 8
N_EXPERTS = N_GROUPS * EXPERTS_PER_GROUP
TOP_K = 2
ROUTE_COLS = 8
LANES = 128
SUBLANES = 8
CONV_HIST = 32
SC_HIST = 8
VMEM_LIMIT = 56 * 1024 * 1024

INPROJ_TM, INPROJ_TN = 1664, 512
CONV_CHUNKS = 13
DELTA_CHUNKS = 5
OUTPROJ_TM, OUTPROJ_TN = 1664, 512
ROUTER_TM = 640
EXPERT_TM = 256
DISPATCH_T = 640
COMBINE_T = 640
DMA_UNROLL = 8
FINAL_T = 512


def _divisor_tile(n, cap, mult):
    best = None
    for t in range(mult, min(n, cap) + 1, mult):
        if n % t == 0:
            best = t
    if best is None:
        raise ValueError(f"no tile for n={n} cap={cap} mult={mult}")
    return best


def _params(sem):
    return pltpu.CompilerParams(dimension_semantics=sem, vmem_limit_bytes=VMEM_LIMIT)


def _dot(a, b, precision=None):
    return jnp.dot(a, b, preferred_element_type=F32, precision=precision)


def _dot_nt(a, b, precision=None):
    return lax.dot_general(a, b, (((1,), (1,)), ((), ())), preferred_element_type=F32, precision=precision)


def _silu(x):
    return x * jax.nn.sigmoid(x)


def _softplus(x):
    return jnp.maximum(x, 0.0) + jnp.log1p(jnp.exp(-jnp.abs(x)))


def _dot_split(a, b, n_parts, split_lhs):
    x = a if split_lhs else b
    acc = None
    for _ in range(n_parts):
        piece = x.astype(BF16)
        term = _dot(piece, b) if split_lhs else _dot(a, piece)
        acc = term if acc is None else acc + term
        x = x - piece.astype(F32)
    return acc


def _pack_bf16_pairs(x):
    n = x.shape[1] // 2
    lo = lax.bitcast_convert_type(x[:, 0:n].astype(F32), jnp.uint32)
    hi = lax.bitcast_convert_type(x[:, n:2 * n].astype(F32), jnp.uint32)
    return (lo >> 16) | hi


def _unpack_bf16_pairs(w):
    lo = lax.bitcast_convert_type(w << 16, F32).astype(BF16)
    hi = lax.bitcast_convert_type(w & jnp.uint32(0xFFFF0000), F32).astype(BF16)
    return lo, hi


def _causal_taps(win, tap_w, n_taps, first_tap):
    acc = jnp.zeros((CHUNK,) + win.shape[1:], F32)
    for res in range(SUBLANES):
        offs = [o for o in range(first_tap, first_tap + n_taps) if o % SUBLANES == res]
        if not offs:
            continue
        shifted = win[res:, :] if res else win
        for o in offs:
            a = o - res
            acc = acc + tap_w(o - first_tap) * shifted[a:a + CHUNK, :]
    return acc


def _inproj_kernel(h_ref, g_ref, w_ref, wbd_ref, p_ref, bdc_ref, bdr_ref, u_sc, *, tm, rb, lp, pad, nb):
    i = pl.program_id(0)
    j = pl.program_id(1)

    @pl.when(j == 0)
    def _():
        wbd = wbd_ref[...].astype(BF16)
        for blk in range(tm // rb):
            rows = slice(blk * rb, (blk + 1) * rb)
            x = h_ref[rows, :]
            ms = jnp.mean(x * x, axis=-1, keepdims=True)
            u = x * lax.rsqrt(ms + EPS) * g_ref[...]
            row = i * tm + blk * rb + lax.broadcasted_iota(jnp.int32, (rb, 1), 0)
            valid = (row >= pad) & (row < lp)
            for b in range(1, nb):
                valid = valid | ((row >= b * lp + pad) & (row < (b + 1) * lp))
            ub = jnp.where(valid, u, 0.0).astype(BF16)
            u_sc[rows, :] = ub
            bdc_ref[rows, :] = _dot_nt(ub, wbd)
            bdr_ref[:, rows] = _dot_nt(wbd, ub)

    p_ref[...] = _dot_nt(u_sc[...], w_ref[...].astype(BF16))


def _inproj(h, norm_g, w_in_t, layer, n_main, lp, pad, nb):
    r, d = h.shape
    n_bd = w_in_t.shape[1] - n_main
    tm = _divisor_tile(r, INPROJ_TM, LANES)
    tn = _divisor_tile(n_main, INPROJ_TN, LANES)
    assert n_main % n_bd == 0 and n_bd % SUBLANES == 0
    kern = functools.partial(_inproj_kernel, tm=tm, rb=LANES, lp=lp, pad=pad, nb=nb)
    return pl.pallas_call(
        kern,
        grid=(r // tm, n_main // tn),
        in_specs=[
            pl.BlockSpec((tm, d), lambda i, j: (i, 0)),
            pl.BlockSpec((1, d), lambda i, j: (0, 0)),
            pl.BlockSpec((None, tn, d), lambda i, j: (layer, j, 0)),
            pl.BlockSpec((None, n_bd, d), lambda i, j: (layer, n_main // n_bd, 0)),
        ],
        out_specs=[
            pl.BlockSpec((tm, tn), lambda i, j: (i, j)),
            pl.BlockSpec((tm, n_bd), lambda i, j: (i, 0)),
            pl.BlockSpec((n_bd, tm), lambda i, j: (0, i)),
        ],
        out_shape=[
            jax.ShapeDtypeStruct((r, n_main), F32),
            jax.ShapeDtypeStruct((r, n_bd), F32),
            jax.ShapeDtypeStruct((n_bd, r), F32),
        ],
        scratch_shapes=[pltpu.VMEM((tm, d), BF16)],
        compiler_params=_params(("arbitrary", "arbitrary")),
        name="inproj",
    )(h, norm_g.reshape(1, d), w_in_t, w_in_t)


def _conv_kernel(a_ref, b_ref, w_ref, bias_ref, lg_ref, lb_ref, o_ref, ybuf, *, t_rows, width):
    t = pl.program_id(1)
    n_blk = t_rows // CHUNK

    @pl.when(t == 0)
    def _():
        ybuf[0:CONV_HIST, :] = jnp.zeros((CONV_HIST, width), F32)

    @pl.when(t > 0)
    def _():
        ybuf[0:CONV_HIST, :] = ybuf[t_rows:t_rows + CONV_HIST, :]

    def glu_body(r, c):
        r0 = pl.multiple_of(r * CHUNK, CHUNK)
        a = a_ref[pl.ds(r0, CHUNK), :]
        g = b_ref[pl.ds(r0, CHUNK), :]
        ybuf[pl.ds(CONV_HIST + r0, CHUNK), :] = a * jax.nn.sigmoid(g)
        return c

    lax.fori_loop(0, n_blk, glu_body, 0)

    first_tap = CONV_HIST - (CONV_KERNEL - 1)

    def body(r, c):
        r0 = pl.multiple_of(r * CHUNK, CHUNK)
        for gi in range(width // LANES):
            ls = slice(gi * LANES, (gi + 1) * LANES)
            win = ybuf[pl.ds(r0, CHUNK + CONV_HIST), ls]
            acc = _causal_taps(win, lambda k: w_ref[k:k + 1, ls], CONV_KERNEL, first_tap)
            y = acc + bias_ref[:, ls]
            mu = jnp.mean(y, axis=-1, keepdims=True)
            dlt = y - mu
            var = jnp.mean(dlt * dlt, axis=-1, keepdims=True)
            yn = dlt * lax.rsqrt(var + EPS) * lg_ref[:, ls] + lb_ref[:, ls]
            o_ref[pl.ds(r0, CHUNK), ls] = _silu(yn).astype(BF16)
        return c

    lax.fori_loop(0, n_blk, body, 0)


def _conformer_conv(p, w_dw, b_dw, ln_g, ln_b, nb, lp, width):
    r = p.shape[0]
    nch = lp // CHUNK
    t_rows = CHUNK * _divisor_tile(nch, CONV_CHUNKS, 1)
    nt = lp // t_rows
    assert width // LANES == CONV_GROUPS
    kern = functools.partial(_conv_kernel, t_rows=t_rows, width=width)
    vec = lambda b, t: (0, 0)
    return pl.pallas_call(
        kern,
        grid=(nb, nt),
        in_specs=[
            pl.BlockSpec((t_rows, width), lambda b, t: (b * nt + t, 0)),
            pl.BlockSpec((t_rows, width), lambda b, t: (b * nt + t, 1)),
            pl.BlockSpec((CONV_KERNEL, width), vec),
            pl.BlockSpec((1, width), vec),
            pl.BlockSpec((1, width), vec),
            pl.BlockSpec((1, width), vec),
        ],
        out_specs=pl.BlockSpec((t_rows, width), lambda b, t: (b * nt + t, 0)),
        out_shape=jax.ShapeDtypeStruct((r, width), BF16),
        scratch_shapes=[pltpu.VMEM((t_rows + CONV_HIST, width), F32)],
        compiler_params=_params(("arbitrary", "arbitrary")),
        name="conformer_conv",
    )(p, p, w_dw, b_dw.reshape(1, width), ln_g.reshape(1, width), ln_b.reshape(1, width))


def _delta_kernel(q_ref, k_ref, v_ref, z_ref, bdc_ref, bdr_ref, scw_ref, alr_ref, dtr_ref, alc_ref, dtc_ref,
                  ng_ref, o_ref, qbuf, kbuf, vbuf, s_sc, *, t_rows, pad, width, nh):
    t = pl.program_id(1)
    n_blk = t_rows // CHUNK
    bufs = (qbuf, kbuf, vbuf)
    srcs = (q_ref, k_ref, v_ref)

    @pl.when(t == 0)
    def _():
        s_sc[...] = jnp.zeros(s_sc.shape, F32)
        for buf in bufs:
            buf[0:SC_HIST, :] = jnp.zeros((SC_HIST, width), F32)

    @pl.when(t > 0)
    def _():
        for buf in bufs:
            buf[0:SC_HIST, :] = buf[t_rows:t_rows + SC_HIST, :]

    def copy_body(r, c):
        r0 = pl.multiple_of(r * CHUNK, CHUNK)
        for buf, src in zip(bufs, srcs):
            buf[pl.ds(SC_HIST + r0, CHUNK), :] = src[pl.ds(r0, CHUNK), :]
        return c

    lax.fori_loop(0, n_blk, copy_body, 0)

    ri = lax.broadcasted_iota(jnp.int32, (CHUNK, CHUNK), 0)
    ci = lax.broadcasted_iota(jnp.int32, (CHUNK, CHUNK), 1)
    incl = ri >= ci
    strict = ri > ci
    tril = incl.astype(BF16)
    triu = (ri <= ci).astype(BF16)
    eye = (ri == ci).astype(F32)

    def head_expand(lanes):
        eh = lax.broadcasted_iota(jnp.int32, (nh, nh * lanes), 0)
        ec = lax.broadcasted_iota(jnp.int32, (nh, nh * lanes), 1)
        return ((ec >= eh * lanes) & (ec < (eh + 1) * lanes)).astype(BF16)

    expand_c = head_expand(CHUNK)
    expand_d = head_expand(HEAD_DIM)
    first_tap = SC_HIST - (SHORT_CONV - 1)
    q_scale = HEAD_DIM ** -0.5
    heads = range(nh)

    def short_conv(buf, part, r0):
        win = buf[pl.ds(r0, CHUNK + SC_HIST), :]
        acc = _causal_taps(win, lambda k: scw_ref[k:k + 1, part * width:(part + 1) * width], SHORT_CONV, first_tap)
        return _silu(acc)

    def body(c, carry):
        r0 = pl.multiple_of(c * CHUNK, CHUNK)
        lrow = t * t_rows + r0 + lax.broadcasted_iota(jnp.int32, (CHUNK, 1), 0)
        lcol = t * t_rows + r0 + lax.broadcasted_iota(jnp.int32, (1, CHUNK), 1)
        bl = bdc_ref[pl.ds(r0, CHUNK), :]
        br = bdr_ref[c]
        beta_col = jnp.where(lrow >= pad, jax.nn.sigmoid(bl[:, 0:nh]), 0.0)
        beta_row = jnp.where(lcol >= pad, jax.nn.sigmoid(br[0:nh, :]), 0.0)
        g_col = jnp.where(lrow >= pad, -jnp.exp(alr_ref[...]) * _softplus(bl[:, nh:2 * nh] + dtr_ref[...]), 0.0)
        g_row = jnp.where(lcol >= pad, -jnp.exp(alc_ref[...]) * _softplus(br[nh:2 * nh, :] + dtc_ref[...]), 0.0)
        gc_col = _dot_split(tril, g_col, 3, split_lhs=False)
        gc_row = _dot_split(g_row, triu, 3, split_lhs=True)
        eg_row = jnp.exp(gc_row)
        ekd_row = jnp.exp(gc_row[:, CHUNK - 1:CHUNK] - gc_row)
        gc_x = _dot_split(gc_col, expand_c, 3, split_lhs=True)
        beta_x = _dot_split(beta_col, expand_c, 2, split_lhs=True)
        eg_x = _dot_split(jnp.exp(gc_col), expand_d, 2, split_lhs=True)

        qc = short_conv(qbuf, 0, r0)
        kc = short_conv(kbuf, 1, r0)
        vc = short_conv(vbuf, 2, r0)
        zc = z_ref[pl.ds(r0, CHUNK), :]

        hd = [slice(h * HEAD_DIM, (h + 1) * HEAD_DIM) for h in heads]
        hc = [slice(h * CHUNK, (h + 1) * CHUNK) for h in heads]
        qn = [qc[:, s] * lax.rsqrt(jnp.sum(qc[:, s] * qc[:, s], axis=-1, keepdims=True) + EPS) * q_scale for s in hd]
        kn = [kc[:, s] * lax.rsqrt(jnp.sum(kc[:, s] * kc[:, s], axis=-1, keepdims=True) + EPS) for s in hd]
        kb16 = [k.astype(BF16) for k in kn]
        v16 = [vc[:, s].astype(BF16) for s in hd]
        decay = [jnp.where(incl, jnp.exp(jnp.where(incl, gc_x[:, hc[h]] - gc_row[h:h + 1, :], 0.0)), 0.0) for h in heads]
        kq = [_dot_nt(jnp.concatenate([kb16[h], qn[h].astype(BF16)], axis=0), kb16[h]) for h in heads]
        xp = [jnp.where(strict, -(kq[h][0:CHUNK] * beta_x[:, hc[h]] * decay[h]), 0.0) for h in heads]
        intra = [kq[h][CHUNK:2 * CHUNK] * decay[h] for h in heads]
        ainv = [eye + x for x in xp]
        n_sq = 1
        while 2 * n_sq < CHUNK:
            xp16 = [x.astype(BF16) for x in xp]
            xp = [_dot(x, x) for x in xp16]
            ainv = [a + _dot(a.astype(BF16), x.astype(BF16)) for a, x in zip(ainv, xp)]
            n_sq *= 2
        u = [_dot((ainv[h] * beta_row[h:h + 1, :]).astype(BF16), v16[h]) for h in heads]
        w = [_dot((ainv[h] * (beta_row[h:h + 1, :] * eg_row[h:h + 1, :])).astype(BF16), kb16[h]) for h in heads]
        q_dec = [qn[h] * eg_x[:, hd[h]] for h in heads]
        s_old = [s_sc[h] for h in heads]
        wq_s = [_dot(jnp.concatenate([w[h], q_dec[h]], axis=0).astype(BF16), s_old[h].astype(BF16)) for h in heads]
        v_new = [u[h] - wq_s[h][0:CHUNK] for h in heads]
        kd_t = [kn[h].T * ekd_row[h:h + 1, :] for h in heads]
        iv = [_dot(jnp.concatenate([intra[h], kd_t[h]], axis=0).astype(BF16), v_new[h].astype(BF16)) for h in heads]
        for h in heads:
            o = wq_s[h][CHUNK:2 * CHUNK] + iv[h][0:CHUNK]
            s_sc[h] = s_old[h] * eg_x[CHUNK - 1:CHUNK, hd[h]] + iv[h][CHUNK:CHUNK + HEAD_DIM]
            on = o * lax.rsqrt(jnp.mean(o * o, axis=-1, keepdims=True) + EPS) * ng_ref[...]
            o_ref[pl.ds(r0, CHUNK), hd[h]] = (on * _silu(zc[:, hd[h]])).astype(BF16)
        return carry

    lax.fori_loop(0, n_blk, body, 0)


def _gated_deltanet(p, bd_col, bd_row3, sc_w, a_log, dt_bias, norm_g, nb, lp, pad, width, col0):
    r = p.shape[0]
    nh = width // HEAD_DIM
    nch = lp // CHUNK
    cpt = _divisor_tile(nch, DELTA_CHUNKS, 1)
    t_rows = CHUNK * cpt
    nt = lp // t_rows
    cb = col0 // width
    assert col0 % width == 0
    kern = functools.partial(_delta_kernel, t_rows=t_rows, pad=pad, width=width, nh=nh)
    vec = lambda b, t: (0, 0)
    part = lambda off: pl.BlockSpec((t_rows, width), lambda b, t: (b * nt + t, cb + off))
    return pl.pallas_call(
        kern,
        grid=(nb, nt),
        in_specs=[
            part(0), part(1), part(2), part(3),
            pl.BlockSpec((t_rows, 2 * nh), lambda b, t: (b * nt + t, 0)),
            pl.BlockSpec((cpt, 2 * nh, CHUNK), lambda b, t: (b * nt + t, 0, 0)),
            pl.BlockSpec((SHORT_CONV, 3 * width), vec),
            pl.BlockSpec((1, nh), vec),
            pl.BlockSpec((1, nh), vec),
            pl.BlockSpec((nh, 1), vec),
            pl.BlockSpec((nh, 1), vec),
            pl.BlockSpec((1, HEAD_DIM), vec),
        ],
        out_specs=pl.BlockSpec((t_rows, width), lambda b, t: (b * nt + t, 0)),
        out_shape=jax.ShapeDtypeStruct((r, width), BF16),
        scratch_shapes=[
            pltpu.VMEM((t_rows + SC_HIST, width), F32),
            pltpu.VMEM((t_rows + SC_HIST, width), F32),
            pltpu.VMEM((t_rows + SC_HIST, width), F32),
            pltpu.VMEM((nh, HEAD_DIM, HEAD_DIM), F32),
        ],
        compiler_params=_params(("arbitrary", "arbitrary")),
        name="gated_deltanet",
    )(p, p, p, p, bd_col, bd_row3, sc_w, a_log.reshape(1, nh), dt_bias.reshape(1, nh),
      a_log.reshape(nh, 1), dt_bias.reshape(nh, 1), norm_g.reshape(1, HEAD_DIM))


def _outproj_kernel(yc_ref, yd_ref, w1_ref, w2_ref, h_ref, o_ref):
    acc = _dot(yc_ref[...], w1_ref[...].astype(BF16))
    acc = acc + _dot(yd_ref[...], w2_ref[...].astype(BF16))
    o_ref[...] = h_ref[...] + acc


def _outproj(y_conv, y_delta, w_out, layer, h):
    r, d = h.shape
    kw = y_conv.shape[1]
    tm = _divisor_tile(r, OUTPROJ_TM, 16)
    tn = _divisor_tile(d, OUTPROJ_TN, LANES)
    return pl.pallas_call(
        _outproj_kernel,
        grid=(r // tm, d // tn),
        in_specs=[
            pl.BlockSpec((tm, kw), lambda i, j: (i, 0)),
            pl.BlockSpec((tm, kw), lambda i, j: (i, 0)),
            pl.BlockSpec((None, kw, tn), lambda i, j: (layer, 0, j)),
            pl.BlockSpec((None, kw, tn), lambda i, j: (layer, 1, j)),
            pl.BlockSpec((tm, tn), lambda i, j: (i, j)),
        ],
        out_specs=pl.BlockSpec((tm, tn), lambda i, j: (i, j)),
        out_shape=jax.ShapeDtypeStruct((r, d), F32),
        compiler_params=_params(("arbitrary", "arbitrary")),
        name="outproj",
    )(y_conv, y_delta, w_out, w_out, h)


def _first_argmax(vals, iota, n):
    m = jnp.max(vals, axis=-1, keepdims=True)
    idx = jnp.min(jnp.where(vals == m, iota, n), axis=-1, keepdims=True)
    return m, idx


def _router_kernel(h_ref, g_ref, w_ref, b_ref, u_ref, route_ref, cnt_ref, carry_sc):
    @pl.when(pl.program_id(0) == 0)
    def _():
        carry_sc[...] = jnp.zeros(carry_sc.shape, F32)

    x = h_ref[...]
    ms = jnp.mean(x * x, axis=-1, keepdims=True)
    u = x * lax.rsqrt(ms + EPS) * g_ref[...]
    w = w_ref[...]
    uh = u.astype(BF16)
    ul = (u - uh.astype(F32)).astype(BF16)
    u_ref[...] = _pack_bf16_pairs(uh)
    wh = w.astype(BF16)
    wl = (w - wh.astype(F32)).astype(BF16)
    logits = _dot(uh, wh) + (_dot(uh, wl) + _dot(ul, wh)) + b_ref[...]
    tm = x.shape[0]
    glog = logits[:, 0:N_GROUPS]
    elog = logits[:, N_GROUPS:N_GROUPS + N_EXPERTS]
    gi = lax.broadcasted_iota(jnp.int32, (tm, N_GROUPS), 1)
    gmax, gsel = _first_argmax(glog, gi, N_GROUPS)
    p_group = 1.0 / jnp.sum(jnp.exp(glog - gmax), axis=-1, keepdims=True)
    ei = lax.broadcasted_iota(jnp.int32, (tm, N_EXPERTS), 1)
    in_group = (ei >= gsel * EXPERTS_PER_GROUP) & (ei < (gsel + 1) * EXPERTS_PER_GROUP)
    neg = jnp.float32(-jnp.inf)
    cand = jnp.where(in_group, elog, neg)
    m1, i1 = _first_argmax(cand, ei, N_EXPERTS)
    cand2 = jnp.where(ei == i1, neg, cand)
    m2, i2 = _first_argmax(cand2, ei, N_EXPERTS)
    e2 = jnp.exp(m2 - m1)
    w1 = p_group / (1.0 + e2)
    w2 = p_group * e2 / (1.0 + e2)
    oh1 = (ei == i1).astype(F32)
    oh2 = (ei == i2).astype(F32)
    oh = oh1 + oh2
    ri = lax.broadcasted_iota(jnp.int32, (tm, tm), 0)
    ci = lax.broadcasted_iota(jnp.int32, (tm, tm), 1)
    before = _dot((ri > ci).astype(BF16), oh.astype(BF16)) + carry_sc[...]
    r1 = jnp.sum(before * oh1, axis=-1, keepdims=True)
    r2 = jnp.sum(before * oh2, axis=-1, keepdims=True)
    carry_sc[...] += jnp.sum(oh, axis=0, keepdims=True)
    cnt_ref[...] = carry_sc[...]
    li = lax.broadcasted_iota(jnp.int32, (tm, ROUTE_COLS), 1)
    rec = jnp.zeros((tm, ROUTE_COLS), F32)
    for k, col in enumerate((i1.astype(F32), i2.astype(F32), r1, r2, w1, w2)):
        rec = jnp.where(li == k, col, rec)
    route_ref[...] = rec


def _router(h, norm_g, w_group, b_group, w_router, b_router):
    r, d = h.shape
    tm = _divisor_tile(r, ROUTER_TM, 16)
    w = jnp.concatenate([w_group, w_router], axis=1)
    b = jnp.concatenate([b_group, b_router]).reshape(1, -1)
    nl = w.shape[1]
    return pl.pallas_call(
        _router_kernel,
        grid=(r // tm,),
        in_specs=[
            pl.BlockSpec((tm, d), lambda i: (i, 0)),
            pl.BlockSpec((1, d), lambda i: (0, 0)),
            pl.BlockSpec((d, nl), lambda i: (0, 0)),
            pl.BlockSpec((1, nl), lambda i: (0, 0)),
        ],
        out_specs=[
            pl.BlockSpec((tm, d // 2), lambda i: (i, 0)),
            pl.BlockSpec((tm, ROUTE_COLS), lambda i: (i, 0)),
            pl.BlockSpec((1, N_EXPERTS), lambda i: (0, 0)),
        ],
        out_shape=[
            jax.ShapeDtypeStruct((r, d // 2), jnp.uint32),
            jax.ShapeDtypeStruct((r, ROUTE_COLS), F32),
            jax.ShapeDtypeStruct((1, N_EXPERTS), F32),
        ],
        scratch_shapes=[pltpu.VMEM((1, N_EXPERTS), F32)],
        compiler_params=_params(("arbitrary",)),
        name="router",
    )(h, norm_g.reshape(1, d), w, b)


def _dispatch_plan(route, cnt, tm, n_tiles):
    cnt = cnt[0].astype(jnp.int32)
    padded = ((cnt + tm - 1) // tm) * tm
    ends = jnp.cumsum(padded)
    off = ends - padded
    onehot = (route[:, 0:TOP_K, None] == jnp.arange(N_EXPERTS, dtype=F32)).astype(F32)
    pos = jnp.einsum("rke,e->rk", onehot, off.astype(F32), precision=HI) + route[:, TOP_K:2 * TOP_K]
    pos = pos.astype(jnp.int32)
    pos0, pos1 = pos[:, 0], pos[:, 1]
    n_used = ends[-1] // tm
    tiles = jnp.arange(n_tiles, dtype=jnp.int32)
    tile_e = jnp.sum((tiles[:, None] * tm >= ends[None, :]).astype(jnp.int32), axis=1)
    tile_e = jnp.minimum(jnp.where(tiles < n_used, tile_e, tile_e[n_used - 1]), N_EXPERTS - 1)
    experts = jnp.arange(N_EXPERTS, dtype=jnp.int32)
    has_rows = cnt > 0
    slot_tab = (jnp.cumsum(has_rows.astype(jnp.int32)) - 1) % 2
    later = jnp.where(has_rows[None, :] & (experts[None, :] > experts[:, None]), experts[None, :], N_EXPERTS)
    next_tab = jnp.min(later, axis=1)
    next_tab = jnp.where(next_tab == N_EXPERTS, -1, next_tab)
    sel = (tile_e[:, None] == experts[None, :]).astype(jnp.int32)
    plan = dict(tile_e=tile_e, slot=jnp.sum(sel * slot_tab[None, :], axis=1), next_e=jnp.sum(sel * next_tab[None, :], axis=1),
                n_used=n_used.reshape(1))
    return pos0, pos1, plan


def _dispatch_kernel(pos0_ref, pos1_ref, u_ref, xs_init_ref, xs_ref, sem, *, t_rows):
    del xs_init_ref
    base = pl.program_id(0) * t_rows

    def body(r, c):
        src = u_ref.at[pl.ds(r, 1)]
        pltpu.make_async_copy(src, xs_ref.at[pl.ds(pos0_ref[base + r], 1)], sem).start()
        pltpu.make_async_copy(src, xs_ref.at[pl.ds(pos1_ref[base + r], 1)], sem).start()
        return c

    lax.fori_loop(0, t_rows, body, 0, unroll=DMA_UNROLL)
    for _ in range(2):
        pltpu.make_async_copy(u_ref, xs_ref.at[pl.ds(0, t_rows)], sem).wait()


def _dispatch(u, pos0, pos1, n_slots):
    r, d = u.shape
    t_rows = _divisor_tile(r, DISPATCH_T, SUBLANES)
    kern = functools.partial(_dispatch_kernel, t_rows=t_rows)
    return pl.pallas_call(
        kern,
        grid_spec=pltpu.PrefetchScalarGridSpec(
            num_scalar_prefetch=2,
            grid=(r // t_rows,),
            in_specs=[
                pl.BlockSpec((t_rows, d), lambda i, p0, p1: (i, 0)),
                pl.BlockSpec(memory_space=pl.ANY),
            ],
            out_specs=pl.BlockSpec(memory_space=pl.ANY),
            scratch_shapes=[pltpu.SemaphoreType.DMA(())],
        ),
        out_shape=jax.ShapeDtypeStruct((n_slots, d), u.dtype),
        input_output_aliases={3: 0},
        compiler_params=_params(("arbitrary",)),
        name="moe_dispatch",
    )(pos0, pos1, u, jnp.zeros((n_slots, d), u.dtype))


def _expert_kernel(te_ref, slot_ref, nxt_ref, nu_ref, x_ref, wg_hbm, wu_hbm, wd_hbm, y_ref,
                   wg_f, wu_f, wd_f, wg_b, wu_b, wd_b, sems, *, layer):
    i = pl.program_id(0)
    e = te_ref[i]
    slot = slot_ref[i]
    used = i < nu_ref[0]
    first_of_expert = used & ((i == 0) | (te_ref[jnp.maximum(i - 1, 0)] != e))

    def weight_copies(expert, s):
        return [pltpu.make_async_copy(src.at[layer, expert], dst.at[s], sems.at[s])
                for src, dst in ((wg_hbm, wg_f), (wu_hbm, wu_f), (wd_hbm, wd_f))]

    @pl.when(i == 0)
    def _():
        for c in weight_copies(e, slot):
            c.start()

    @pl.when(first_of_expert)
    def _():
        for c in weight_copies(e, slot):
            c.wait()
        nxt = nxt_ref[i]

        @pl.when(nxt >= 0)
        def _():
            for c in weight_copies(nxt, 1 - slot):
                c.start()

        wg_b[...] = wg_f[slot].astype(BF16)
        wu_b[...] = wu_f[slot].astype(BF16)
        wd_b[...] = wd_f[slot].astype(BF16)

    @pl.when(used)
    def _():
        xa, xb = _unpack_bf16_pairs(x_ref[...])
        half = xa.shape[1]
        hg = _dot(xa, wg_b[0:half, :]) + _dot(xb, wg_b[half:2 * half, :])
        hu = _dot(xa, wu_b[0:half, :]) + _dot(xb, wu_b[half:2 * half, :])
        hid = (_silu(hg) * hu).astype(BF16)
        y_ref[...] = _dot(hid, wd_b[...])

    @pl.when(jnp.logical_not(used))
    def _():
        y_ref[...] = jnp.zeros(y_ref.shape, F32)


def _experts(xs, plan, w_gate, w_up, w_down, layer, tm):
    n_slots = xs.shape[0]
    _, _, d, f = w_gate.shape
    assert xs.shape[1] * 2 == d
    n_tiles = n_slots // tm
    used_row = lambda i, te, sl, nx, nu: (jnp.minimum(i, nu[0] - 1), 0)
    hbm = pl.BlockSpec(memory_space=pl.ANY)
    kern = functools.partial(_expert_kernel, layer=layer)
    return pl.pallas_call(
        kern,
        grid_spec=pltpu.PrefetchScalarGridSpec(
            num_scalar_prefetch=4,
            grid=(n_tiles,),
            in_specs=[pl.BlockSpec((tm, d // 2), used_row), hbm, hbm, hbm],
            out_specs=pl.BlockSpec((tm, d), lambda i, te, sl, nx, nu: (i, 0)),
            scratch_shapes=[
                pltpu.VMEM((2, d, f), F32), pltpu.VMEM((2, d, f), F32), pltpu.VMEM((2, f, d), F32),
                pltpu.VMEM((d, f), BF16), pltpu.VMEM((d, f), BF16), pltpu.VMEM((f, d), BF16),
                pltpu.SemaphoreType.DMA((2,)),
            ],
        ),
        out_shape=jax.ShapeDtypeStruct((n_slots, d), F32),
        compiler_params=_params(("arbitrary",)),
        name="moe_experts",
    )(plan["tile_e"], plan["slot"], plan["next_e"], plan["n_used"], xs, w_gate, w_up, w_down)


def _combine_kernel(pos0_ref, pos1_ref, h_ref, route_ref, ys_ref, o_ref, buf0, buf1, sem, *, t_rows):
    base = pl.program_id(0) * t_rows

    def body(r, c):
        pltpu.make_async_copy(ys_ref.at[pl.ds(pos0_ref[base + r], 1)], buf0.at[pl.ds(r, 1)], sem).start()
        pltpu.make_async_copy(ys_ref.at[pl.ds(pos1_ref[base + r], 1)], buf1.at[pl.ds(r, 1)], sem).start()
        return c

    lax.fori_loop(0, t_rows, body, 0, unroll=DMA_UNROLL)
    for buf in (buf0, buf1):
        pltpu.make_async_copy(ys_ref.at[pl.ds(0, t_rows)], buf, sem).wait()
    g = route_ref[...]
    o_ref[...] = h_ref[...] + g[:, 4:5] * buf0[...] + g[:, 5:6] * buf1[...]


def _combine(h, route, ys, pos0, pos1):
    r, d = h.shape
    t_rows = _divisor_tile(r, COMBINE_T, SUBLANES)
    kern = functools.partial(_combine_kernel, t_rows=t_rows)
    return pl.pallas_call(
        kern,
        grid_spec=pltpu.PrefetchScalarGridSpec(
            num_scalar_prefetch=2,
            grid=(r // t_rows,),
            in_specs=[
                pl.BlockSpec((t_rows, d), lambda i, p0, p1: (i, 0)),
                pl.BlockSpec((t_rows, ROUTE_COLS), lambda i, p0, p1: (i, 0)),
                pl.BlockSpec(memory_space=pl.ANY),
            ],
            out_specs=pl.BlockSpec((t_rows, d), lambda i, p0, p1: (i, 0)),
            scratch_shapes=[pltpu.VMEM((t_rows, d), F32), pltpu.VMEM((t_rows, d), F32), pltpu.SemaphoreType.DMA(())],
        ),
        out_shape=jax.ShapeDtypeStruct((r, d), F32),
        compiler_params=_params(("arbitrary",)),
        name="moe_combine",
    )(pos0, pos1, h, route, ys)


def _hier_moe(h, norm_g, w_group, b_group, w_router, b_router, w_gate, w_up, w_down, layer):
    r = h.shape[0]
    tm = EXPERT_TM
    n_tiles = -(-(TOP_K * r + N_EXPERTS * (tm - 1)) // tm)
    u, route, cnt = _router(h, norm_g, w_group, b_group, w_router, b_router)
    pos0, pos1, plan = _dispatch_plan(route, cnt, tm, n_tiles)
    xs = _dispatch(u, pos0, pos1, n_tiles * tm)
    ys = _experts(xs, plan, w_gate, w_up, w_down, layer, tm)
    return _combine(h, route, ys, pos0, pos1)


def _final_kernel(h_ref, g_ref, o_ref):
    x = h_ref[...]
    ms = jnp.mean(x * x, axis=-1, keepdims=True)
    o_ref[0] = x * lax.rsqrt(ms + EPS) * g_ref[...]


def _final_norm(h, norm_g, nb, lp, seq, skip):
    d = h.shape[1]
    t_rows = _divisor_tile(seq, FINAL_T, SUBLANES)
    assert skip % SUBLANES == 0 and lp % SUBLANES == 0
    first_row = lambda b, t: (pl.multiple_of(b * lp + skip + t * t_rows, SUBLANES), 0)
    return pl.pallas_call(
        _final_kernel,
        grid=(nb, seq // t_rows),
        in_specs=[
            pl.BlockSpec((pl.Element(t_rows), pl.Element(d)), first_row),
            pl.BlockSpec((1, d), lambda b, t: (0, 0)),
        ],
        out_specs=pl.BlockSpec((1, t_rows, d), lambda b, t: (b, t, 0)),
        out_shape=jax.ShapeDtypeStruct((nb, seq, d), F32),
        compiler_params=_params(("arbitrary", "arbitrary")),
        name="final_norm",
    )(h, norm_g.reshape(1, d))


def kernel(x, meta, attn_norm, w_in, conv_dw_w, conv_dw_b, conv_ln_g, conv_ln_b, short_conv_w, a_log, dt_bias,
           delta_norm_g, w_out, ffn_norm, w_group, b_group, w_router, b_router, w_gate, w_up, w_down, final_norm):
    nb, seq, d = x.shape
    depth = w_in.shape[0]
    conv_w = conv_dw_w.shape[2]
    delta_w = short_conv_w.shape[2] // 3
    nh = delta_w // HEAD_DIM
    n_main = 2 * conv_w + 4 * delta_w
    assert w_in.shape[2] == n_main + 2 * nh and conv_w == delta_w
    ln = N_META + seq
    pad = (-ln) % CHUNK
    lp = ln + pad
    skip = pad + N_META

    meta_b = jnp.broadcast_to(meta[None].astype(x.dtype), (nb, N_META, d))
    h = jnp.concatenate([jnp.zeros((nb, pad, d), x.dtype), meta_b, x], axis=1).reshape(nb * lp, d)

    w_in_t = jnp.swapaxes(w_in, 1, 2)
    for l in range(depth):
        p, bd_col, bd_row = _inproj(h, attn_norm[l], w_in_t, l, n_main, lp, pad, nb)
        bd_row3 = bd_row.reshape(2 * nh, nb * lp // CHUNK, CHUNK).transpose(1, 0, 2)
        y_conv = _conformer_conv(p, conv_dw_w[l], conv_dw_b[l], conv_ln_g[l], conv_ln_b[l], nb, lp, conv_w)
        y_delta = _gated_deltanet(p, bd_col, bd_row3, short_conv_w[l], a_log[l], dt_bias[l], delta_norm_g[l],
                                  nb, lp, pad, delta_w, 2 * conv_w)
        h = _outproj(y_conv, y_delta, w_out, l, h)
        h = _hier_moe(h, ffn_norm[l], w_group[l], b_group[l], w_router[l], b_router[l], w_gate, w_up, w_down, l)
    return _final_norm(h, final_norm, nb, lp, seq, skip)
```

```python
import functools

import jax
import jax.numpy as jnp
from jax import lax
from jax.experimental import pallas as pl
from jax.experimental.pallas import tpu as pltpu

F32 = jnp.float32
BF16 = jnp.bfloat16
HI = lax.Precision.HIGHEST

EPS = 1e-6
CHUNK = 64
N_META = 16
CONV_GROUPS = 8
CONV_KERNEL = 31
HEAD_DIM = 128
SHORT_CONV = 4
N_GROUPS = 4
EXPERTS_PER_GROUP = 8
N_EXPERTS = N_GROUPS * EXPERTS_PER_GROUP
TOP_K = 2
ROUTE_COLS = 8
LANES = 128
SUBLANES = 8
CONV_HIST = 32
SC_HIST = 8
VMEM_LIMIT = 56 * 1024 * 1024

INPROJ_TM, INPROJ_TN = 1664, 512
CONV_CHUNKS = 13
DELTA_CHUNKS = 5
DELTA_INTERLEAVE = 3
OUTPROJ_TM, OUTPROJ_TN = 1664, 512
ROUTER_TM = 640
EXPERT_TM = 256
DISPATCH_T = 640
COMBINE_T = 640
DMA_UNROLL = 8
FINAL_T = 512


def _divisor_tile(n, cap, mult):
    best = None
    for t in range(mult, min(n, cap) + 1, mult):
        if n % t == 0:
            best = t
    if best is None:
        raise ValueError(f"no tile for n={n} cap={cap} mult={mult}")
    return best


def _params(sem):
    return pltpu.CompilerParams(dimension_semantics=sem, vmem_limit_bytes=VMEM_LIMIT)


def _dot(a, b, precision=None):
    return jnp.dot(a, b, preferred_element_type=F32, precision=precision)


def _dot_nt(a, b, precision=None):
    return lax.dot_general(a, b, (((1,), (1,)), ((), ())), preferred_element_type=F32, precision=precision)


def _silu(x):
    return x * jax.nn.sigmoid(x)


def _softplus(x):
    return jnp.maximum(x, 0.0) + jnp.log1p(jnp.exp(-jnp.abs(x)))


def _dot_split(a, b, n_parts, split_lhs):
    x = a if split_lhs else b
    acc = None
    for _ in range(n_parts):
        piece = x.astype(BF16)
        term = _dot(piece, b) if split_lhs else _dot(a, piece)
        acc = term if acc is None else acc + term
        x = x - piece.astype(F32)
    return acc


def _pack_bf16_pairs(x):
    n = x.shape[1] // 2
    lo = lax.bitcast_convert_type(x[:, 0:n].astype(F32), jnp.uint32)
    hi = lax.bitcast_convert_type(x[:, n:2 * n].astype(F32), jnp.uint32)
    return (lo >> 16) | hi


def _unpack_bf16_pairs(w):
    lo = lax.bitcast_convert_type(w << 16, F32).astype(BF16)
    hi = lax.bitcast_convert_type(w & jnp.uint32(0xFFFF0000), F32).astype(BF16)
    return lo, hi


def _causal_taps(win, tap_w, n_taps, first_tap):
    rows = win.shape[0]
    acc = jnp.zeros((CHUNK,) + win.shape[1:], F32)
    for res in range(SUBLANES):
        offs = [o for o in range(first_tap, first_tap + n_taps) if o % SUBLANES == res]
        if not offs:
            continue
        shifted = pltpu.roll(win, rows - res, axis=0) if res else win
        for o in offs:
            assert o + CHUNK <= rows
            a = o - res
            acc = acc + tap_w(o - first_tap) * shifted[a:a + CHUNK, :]
    return acc


def _inproj_kernel(h_ref, g_ref, w_ref, wbd_ref, p_ref, bdc_ref, bdr_ref, u_sc, *, tm, rb, lp, pad, nb):
    i = pl.program_id(0)
    j = pl.program_id(1)

    @pl.when(j == 0)
    def _():
        wbd = wbd_ref[...].astype(BF16)
        for blk in range(tm // rb):
            rows = slice(blk * rb, (blk + 1) * rb)
            x = h_ref[rows, :]
            ms = jnp.mean(x * x, axis=-1, keepdims=True)
            u = x * lax.rsqrt(ms + EPS) * g_ref[...]
            row = i * tm + blk * rb + lax.broadcasted_iota(jnp.int32, (rb, 1), 0)
            valid = (row >= pad) & (row < lp)
            for b in range(1, nb):
                valid = valid | ((row >= b * lp + pad) & (row < (b + 1) * lp))
            ub = jnp.where(valid, u, 0.0).astype(BF16)
            u_sc[rows, :] = ub
            bdc_ref[rows, :] = _dot_nt(ub, wbd)
            bdr_ref[:, rows] = _dot_nt(wbd, ub)

    p_ref[...] = _dot_nt(u_sc[...], w_ref[...].astype(BF16))


def _inproj(h, norm_g, w_in_t, layer, n_main, lp, pad, nb):
    r, d = h.shape
    n_bd = w_in_t.shape[1] - n_main
    tm = _divisor_tile(r, INPROJ_TM, LANES)
    tn = _divisor_tile(n_main, INPROJ_TN, LANES)
    assert n_main % n_bd == 0 and n_bd % SUBLANES == 0
    kern = functools.partial(_inproj_kernel, tm=tm, rb=LANES, lp=lp, pad=pad, nb=nb)
    return pl.pallas_call(
        kern,
        grid=(r // tm, n_main // tn),
        in_specs=[
            pl.BlockSpec((tm, d), lambda i, j: (i, 0)),
            pl.BlockSpec((1, d), lambda i, j: (0, 0)),
            pl.BlockSpec((None, tn, d), lambda i, j: (layer, j, 0)),
            pl.BlockSpec((None, n_bd, d), lambda i, j: (layer, n_main // n_bd, 0)),
        ],
        out_specs=[
            pl.BlockSpec((tm, tn), lambda i, j: (i, j)),
            pl.BlockSpec((tm, n_bd), lambda i, j: (i, 0)),
            pl.BlockSpec((n_bd, tm), lambda i, j: (0, i)),
        ],
        out_shape=[
            jax.ShapeDtypeStruct((r, n_main), F32),
            jax.ShapeDtypeStruct((r, n_bd), F32),
            jax.ShapeDtypeStruct((n_bd, r), F32),
        ],
        scratch_shapes=[pltpu.VMEM((tm, d), BF16)],
        compiler_params=_params(("arbitrary", "arbitrary")),
        name="inproj",
    )(h, norm_g.reshape(1, d), w_in_t, w_in_t)


def _conv_kernel(a_ref, b_ref, w_ref, bias_ref, lg_ref, lb_ref, o_ref, ybuf, *, t_rows, width):
    t = pl.program_id(1)
    n_blk = t_rows // CHUNK

    @pl.when(t == 0)
    def _():
        ybuf[0:CONV_HIST, :] = jnp.zeros((CONV_HIST, width), F32)

    @pl.when(t > 0)
    def _():
        ybuf[0:CONV_HIST, :] = ybuf[t_rows:t_rows + CONV_HIST, :]

    def glu_body(r, c):
        r0 = pl.multiple_of(r * CHUNK, CHUNK)
        a = a_ref[pl.ds(r0, CHUNK), :]
        g = b_ref[pl.ds(r0, CHUNK), :]
        ybuf[pl.ds(CONV_HIST + r0, CHUNK), :] = a * jax.nn.sigmoid(g)
        return c

    lax.fori_loop(0, n_blk, glu_body, 0)

    first_tap = CONV_HIST - (CONV_KERNEL - 1)

    def body(r, c):
        r0 = pl.multiple_of(r * CHUNK, CHUNK)
        for gi in range(width // LANES):
            ls = slice(gi * LANES, (gi + 1) * LANES)
            win = ybuf[pl.ds(r0, CHUNK + CONV_HIST), ls]
            acc = _causal_taps(win, lambda k: w_ref[k:k + 1, ls], CONV_KERNEL, first_tap)
            y = acc + bias_ref[:, ls]
            mu = jnp.mean(y, axis=-1, keepdims=True)
            dlt = y - mu
            var = jnp.mean(dlt * dlt, axis=-1, keepdims=True)
            yn = dlt * lax.rsqrt(var + EPS) * lg_ref[:, ls] + lb_ref[:, ls]
            o_ref[pl.ds(r0, CHUNK), ls] = _silu(yn).astype(BF16)
        return c

    lax.fori_loop(0, n_blk, body, 0)


def _conformer_conv(p, w_dw, b_dw, ln_g, ln_b, nb, lp, width):
    r = p.shape[0]
    nch = lp // CHUNK
    t_rows = CHUNK * _divisor_tile(nch, CONV_CHUNKS, 1)
    nt = lp // t_rows
    assert width // LANES == CONV_GROUPS
    kern = functools.partial(_conv_kernel, t_rows=t_rows, width=width)
    vec = lambda b, t: (0, 0)
    return pl.pallas_call(
        kern,
        grid=(nb, nt),
        in_specs=[
            pl.BlockSpec((t_rows, width), lambda b, t: (b * nt + t, 0)),
            pl.BlockSpec((t_rows, width), lambda b, t: (b * nt + t, 1)),
            pl.BlockSpec((CONV_KERNEL, width), vec),
            pl.BlockSpec((1, width), vec),
            pl.BlockSpec((1, width), vec),
            pl.BlockSpec((1, width), vec),
        ],
        out_specs=pl.BlockSpec((t_rows, width), lambda b, t: (b * nt + t, 0)),
        out_shape=jax.ShapeDtypeStruct((r, width), BF16),
        scratch_shapes=[pltpu.VMEM((t_rows + CONV_HIST, width), F32)],
        compiler_params=_params(("arbitrary", "arbitrary")),
        name="conformer_conv",
    )(p, p, w_dw, b_dw.reshape(1, width), ln_g.reshape(1, width), ln_b.reshape(1, width))


def _delta_kernel(q_ref, k_ref, v_ref, z_ref, bdc_ref, bdr_ref, scw_ref, alr_ref, dtr_ref, alc_ref, dtc_ref,
                  ng_ref, o_ref, qbuf, kbuf, vbuf, s_sc, *, t_rows, pad, width, nh):
    t = pl.program_id(1)
    n_blk = t_rows // CHUNK
    bufs = (qbuf, kbuf, vbuf)
    srcs = (q_ref, k_ref, v_ref)

    @pl.when(t == 0)
    def _():
        s_sc[...] = jnp.zeros(s_sc.shape, F32)
        for buf in bufs:
            buf[0:SC_HIST, :] = jnp.zeros((SC_HIST, width), F32)

    @pl.when(t > 0)
    def _():
        for buf in bufs:
            buf[0:SC_HIST, :] = buf[t_rows:t_rows + SC_HIST, :]

    def copy_body(r, c):
        r0 = pl.multiple_of(r * CHUNK, CHUNK)
        for buf, src in zip(bufs, srcs):
            buf[pl.ds(SC_HIST + r0, CHUNK), :] = src[pl.ds(r0, CHUNK), :]
        return c

    lax.fori_loop(0, n_blk, copy_body, 0)

    ri = lax.broadcasted_iota(jnp.int32, (CHUNK, CHUNK), 0)
    ci = lax.broadcasted_iota(jnp.int32, (CHUNK, CHUNK), 1)
    incl = ri >= ci
    strict = ri > ci
    tril = incl.astype(BF16)
    triu = (ri <= ci).astype(BF16)
    eye = (ri == ci).astype(F32)

    def head_expand(lanes):
        eh = lax.broadcasted_iota(jnp.int32, (nh, nh * lanes), 0)
        ec = lax.broadcasted_iota(jnp.int32, (nh, nh * lanes), 1)
        return ((ec >= eh * lanes) & (ec < (eh + 1) * lanes)).astype(BF16)

    expand_c = head_expand(CHUNK)
    expand_d = head_expand(HEAD_DIM)
    first_tap = SC_HIST - (SHORT_CONV - 1)
    q_scale = HEAD_DIM ** -0.5
    heads = range(nh)

    def short_conv(buf, part, r0):
        win = buf[pl.ds(r0, CHUNK + SC_HIST), :]
        acc = _causal_taps(win, lambda k: scw_ref[k:k + 1, part * width:(part + 1) * width], SHORT_CONV, first_tap)
        return _silu(acc)

    hd = [slice(h * HEAD_DIM, (h + 1) * HEAD_DIM) for h in heads]
    hc = [slice(h * CHUNK, (h + 1) * CHUNK) for h in heads]

    def chunk_inputs(c):
        r0 = pl.multiple_of(c * CHUNK, CHUNK)
        lrow = t * t_rows + r0 + lax.broadcasted_iota(jnp.int32, (CHUNK, 1), 0)
        lcol = t * t_rows + r0 + lax.broadcasted_iota(jnp.int32, (1, CHUNK), 1)
        bl = bdc_ref[pl.ds(r0, CHUNK), :]
        br = bdr_ref[c]
        beta_col = jnp.where(lrow >= pad, jax.nn.sigmoid(bl[:, 0:nh]), 0.0)
        beta_row = jnp.where(lcol >= pad, jax.nn.sigmoid(br[0:nh, :]), 0.0)
        g_col = jnp.where(lrow >= pad, -jnp.exp(alr_ref[...]) * _softplus(bl[:, nh:2 * nh] + dtr_ref[...]), 0.0)
        g_row = jnp.where(lcol >= pad, -jnp.exp(alc_ref[...]) * _softplus(br[nh:2 * nh, :] + dtc_ref[...]), 0.0)
        gc_col = _dot_split(tril, g_col, 3, split_lhs=False)
        gc_row = _dot_split(g_row, triu, 3, split_lhs=True)
        gc_x = _dot_split(gc_col, expand_c, 3, split_lhs=True)
        qc = short_conv(qbuf, 0, r0)
        kc = short_conv(kbuf, 1, r0)
        vc = short_conv(vbuf, 2, r0)
        qn = [qc[:, s] * lax.rsqrt(jnp.sum(qc[:, s] * qc[:, s], axis=-1, keepdims=True) + EPS) * q_scale for s in hd]
        kn = [kc[:, s] * lax.rsqrt(jnp.sum(kc[:, s] * kc[:, s], axis=-1, keepdims=True) + EPS) for s in hd]
        return dict(
            r0=r0, beta_row=beta_row, eg_row=jnp.exp(gc_row), ekd_row=jnp.exp(gc_row[:, CHUNK - 1:CHUNK] - gc_row),
            beta_x=_dot_split(beta_col, expand_c, 2, split_lhs=True),
            eg_x=_dot_split(jnp.exp(gc_col), expand_d, 2, split_lhs=True),
            qn=qn, kn=kn, k16=[k.astype(BF16) for k in kn], v16=[vc[:, s].astype(BF16) for s in hd],
            decay=[jnp.where(incl, jnp.exp(jnp.where(incl, gc_x[:, hc[h]] - gc_row[h:h + 1, :], 0.0)), 0.0) for h in heads],
            zc=z_ref[pl.ds(r0, CHUNK), :])

    def process(chunks):
        cin = [chunk_inputs(c) for c in chunks]
        pairs = [(ci, h) for ci in range(len(chunks)) for h in heads]
        kq = [_dot_nt(jnp.concatenate([cin[ci]["k16"][h], cin[ci]["qn"][h].astype(BF16)], axis=0), cin[ci]["k16"][h])
              for ci, h in pairs]
        xp = [jnp.where(strict, -(kq[n][0:CHUNK] * cin[ci]["beta_x"][:, hc[h]] * cin[ci]["decay"][h]), 0.0)
              for n, (ci, h) in enumerate(pairs)]
        intra = [kq[n][CHUNK:2 * CHUNK] * cin[ci]["decay"][h] for n, (ci, h) in enumerate(pairs)]
        ainv = [eye + x for x in xp]
        n_sq = 1
        while 2 * n_sq < CHUNK:
            xp16 = [x.astype(BF16) for x in xp]
            xp = [_dot(x, x) for x in xp16]
            ainv = [a + _dot(a.astype(BF16), x.astype(BF16)) for a, x in zip(ainv, xp)]
            n_sq *= 2
        u = [_dot((ainv[n] * cin[ci]["beta_row"][h:h + 1, :]).astype(BF16), cin[ci]["v16"][h]) for n, (ci, h) in enumerate(pairs)]
        w = [_dot((ainv[n] * (cin[ci]["beta_row"][h:h + 1, :] * cin[ci]["eg_row"][h:h + 1, :])).astype(BF16), cin[ci]["k16"][h])
             for n, (ci, h) in enumerate(pairs)]
        for ci, cc in enumerate(cin):
            base = ci * nh
            q_dec = [cc["qn"][h] * cc["eg_x"][:, hd[h]] for h in heads]
            kd_t = [cc["kn"][h].T * cc["ekd_row"][h:h + 1, :] for h in heads]
            s_old = [s_sc[h] for h in heads]
            wq_s = [_dot(jnp.concatenate([w[base + h], q_dec[h]], axis=0).astype(BF16), s_old[h].astype(BF16)) for h in heads]
            v_new = [u[base + h] - wq_s[h][0:CHUNK] for h in heads]
            iv = [_dot(jnp.concatenate([intra[base + h], kd_t[h]], axis=0).astype(BF16), v_new[h].astype(BF16)) for h in heads]
            for h in heads:
                o = wq_s[h][CHUNK:2 * CHUNK] + iv[h][0:CHUNK]
                s_sc[h] = s_old[h] * cc["eg_x"][CHUNK - 1:CHUNK, hd[h]] + iv[h][CHUNK:CHUNK + HEAD_DIM]
                on = o * lax.rsqrt(jnp.mean(o * o, axis=-1, keepdims=True) + EPS) * ng_ref[...]
                o_ref[pl.ds(cc["r0"], CHUNK), hd[h]] = (on * _silu(cc["zc"][:, hd[h]])).astype(BF16)

    n_grp = n_blk // DELTA_INTERLEAVE

    def body(g, carry):
        process([g * DELTA_INTERLEAVE + k for k in range(DELTA_INTERLEAVE)])
        return carry

    lax.fori_loop(0, n_grp, body, 0)
    if n_blk % DELTA_INTERLEAVE:
        process(list(range(n_grp * DELTA_INTERLEAVE, n_blk)))


def _gated_deltanet(p, bd_col, bd_row3, sc_w, a_log, dt_bias, norm_g, nb, lp, pad, width, col0):
    r = p.shape[0]
    nh = width // HEAD_DIM
    nch = lp // CHUNK
    cpt = _divisor_tile(nch, DELTA_CHUNKS, 1)
    t_rows = CHUNK * cpt
    nt = lp // t_rows
    cb = col0 // width
    assert col0 % width == 0
    kern = functools.partial(_delta_kernel, t_rows=t_rows, pad=pad, width=width, nh=nh)
    vec = lambda b, t: (0, 0)
    part = lambda off: pl.BlockSpec((t_rows, width), lambda b, t: (b * nt + t, cb + off))
    return pl.pallas_call(
        kern,
        grid=(nb, nt),
        in_specs=[
            part(0), part(1), part(2), part(3),
            pl.BlockSpec((t_rows, 2 * nh), lambda b, t: (b * nt + t, 0)),
            pl.BlockSpec((cpt, 2 * nh, CHUNK), lambda b, t: (b * nt + t, 0, 0)),
            pl.BlockSpec((SHORT_CONV, 3 * width), vec),
            pl.BlockSpec((1, nh), vec),
            pl.BlockSpec((1, nh), vec),
            pl.BlockSpec((nh, 1), vec),
            pl.BlockSpec((nh, 1), vec),
            pl.BlockSpec((1, HEAD_DIM), vec),
        ],
        out_specs=pl.BlockSpec((t_rows, width), lambda b, t: (b * nt + t, 0)),
        out_shape=jax.ShapeDtypeStruct((r, width), BF16),
        scratch_shapes=[
            pltpu.VMEM((t_rows + SC_HIST, width), F32),
            pltpu.VMEM((t_rows + SC_HIST, width), F32),
            pltpu.VMEM((t_rows + SC_HIST, width), F32),
            pltpu.VMEM((nh, HEAD_DIM, HEAD_DIM), F32),
        ],
        compiler_params=_params(("arbitrary", "arbitrary")),
        name="gated_deltanet",
    )(p, p, p, p, bd_col, bd_row3, sc_w, a_log.reshape(1, nh), dt_bias.reshape(1, nh),
      a_log.reshape(nh, 1), dt_bias.reshape(nh, 1), norm_g.reshape(1, HEAD_DIM))


def _outproj_kernel(yc_ref, yd_ref, w1_ref, w2_ref, h_ref, o_ref):
    acc = _dot(yc_ref[...], w1_ref[...].astype(BF16))
    acc = acc + _dot(yd_ref[...], w2_ref[...].astype(BF16))
    o_ref[...] = h_ref[...] + acc


def _outproj(y_conv, y_delta, w_out, layer, h):
    r, d = h.shape
    kw = y_conv.shape[1]
    tm = _divisor_tile(r, OUTPROJ_TM, 16)
    tn = _divisor_tile(d, OUTPROJ_TN, LANES)
    return pl.pallas_call(
        _outproj_kernel,
        grid=(r // tm, d // tn),
        in_specs=[
            pl.BlockSpec((tm, kw), lambda i, j: (i, 0)),
            pl.BlockSpec((tm, kw), lambda i, j: (i, 0)),
            pl.BlockSpec((None, kw, tn), lambda i, j: (layer, 0, j)),
            pl.BlockSpec((None, kw, tn), lambda i, j: (layer, 1, j)),
            pl.BlockSpec((tm, tn), lambda i, j: (i, j)),
        ],
        out_specs=pl.BlockSpec((tm, tn), lambda i, j: (i, j)),
        out_shape=jax.ShapeDtypeStruct((r, d), F32),
        compiler_params=_params(("arbitrary", "arbitrary")),
        name="outproj",
    )(y_conv, y_delta, w_out, w_out, h)


def _first_argmax(vals, iota, n):
    m = jnp.max(vals, axis=-1, keepdims=True)
    idx = jnp.min(jnp.where(vals == m, iota, n), axis=-1, keepdims=True)
    return m, idx


def _router_kernel(h_ref, g_ref, w_ref, b_ref, u_ref, route_ref, cnt_ref, carry_sc):
    @pl.when(pl.program_id(0) == 0)
    def _():
        carry_sc[...] = jnp.zeros(carry_sc.shape, F32)

    x = h_ref[...]
    ms = jnp.mean(x * x, axis=-1, keepdims=True)
    u = x * lax.rsqrt(ms + EPS) * g_ref[...]
    w = w_ref[...]
    uh = u.astype(BF16)
    ul = (u - uh.astype(F32)).astype(BF16)
    u_ref[...] = _pack_bf16_pairs(uh)
    wh = w.astype(BF16)
    wl = (w - wh.astype(F32)).astype(BF16)
    logits = _dot(uh, wh) + (_dot(uh, wl) + _dot(ul, wh)) + b_ref[...]
    tm = x.shape[0]
    glog = logits[:, 0:N_GROUPS]
    elog = logits[:, N_GROUPS:N_GROUPS + N_EXPERTS]
    gi = lax.broadcasted_iota(jnp.int32, (tm, N_GROUPS), 1)
    gmax, gsel = _first_argmax(glog, gi, N_GROUPS)
    p_group = 1.0 / jnp.sum(jnp.exp(glog - gmax), axis=-1, keepdims=True)
    ei = lax.broadcasted_iota(jnp.int32, (tm, N_EXPERTS), 1)
    in_group = (ei >= gsel * EXPERTS_PER_GROUP) & (ei < (gsel + 1) * EXPERTS_PER_GROUP)
    neg = jnp.float32(-jnp.inf)
    cand = jnp.where(in_group, elog, neg)
    m1, i1 = _first_argmax(cand, ei, N_EXPERTS)
    cand2 = jnp.where(ei == i1, neg, cand)
    m2, i2 = _first_argmax(cand2, ei, N_EXPERTS)
    e2 = jnp.exp(m2 - m1)
    w1 = p_group / (1.0 + e2)
    w2 = p_group * e2 / (1.0 + e2)
    oh1 = (ei == i1).astype(F32)
    oh2 = (ei == i2).astype(F32)
    oh = oh1 + oh2
    ri = lax.broadcasted_iota(jnp.int32, (tm, tm), 0)
    ci = lax.broadcasted_iota(jnp.int32, (tm, tm), 1)
    before = _dot((ri > ci).astype(BF16), oh.astype(BF16)) + carry_sc[...]
    r1 = jnp.sum(before * oh1, axis=-1, keepdims=True)
    r2 = jnp.sum(before * oh2, axis=-1, keepdims=True)
    carry_sc[...] += jnp.sum(oh, axis=0, keepdims=True)
    cnt_ref[...] = carry_sc[...]
    li = lax.broadcasted_iota(jnp.int32, (tm, ROUTE_COLS), 1)
    rec = jnp.zeros((tm, ROUTE_COLS), F32)
    for k, col in enumerate((i1.astype(F32), i2.astype(F32), r1, r2, w1, w2)):
        rec = jnp.where(li == k, col, rec)
    route_ref[...] = rec


def _router(h, norm_g, w_group, b_group, w_router, b_router):
    r, d = h.shape
    tm = _divisor_tile(r, ROUTER_TM, 16)
    w = jnp.concatenate([w_group, w_router], axis=1)
    b = jnp.concatenate([b_group, b_router]).reshape(1, -1)
    nl = w.shape[1]
    return pl.pallas_call(
        _router_kernel,
        grid=(r // tm,),
        in_specs=[
            pl.BlockSpec((tm, d), lambda i: (i, 0)),
            pl.BlockSpec((1, d), lambda i: (0, 0)),
            pl.BlockSpec((d, nl), lambda i: (0, 0)),
            pl.BlockSpec((1, nl), lambda i: (0, 0)),
        ],
        out_specs=[
            pl.BlockSpec((tm, d // 2), lambda i: (i, 0)),
            pl.BlockSpec((tm, ROUTE_COLS), lambda i: (i, 0)),
            pl.BlockSpec((1, N_EXPERTS), lambda i: (0, 0)),
        ],
        out_shape=[
            jax.ShapeDtypeStruct((r, d // 2), jnp.uint32),
            jax.ShapeDtypeStruct((r, ROUTE_COLS), F32),
            jax.ShapeDtypeStruct((1, N_EXPERTS), F32),
        ],
        scratch_shapes=[pltpu.VMEM((1, N_EXPERTS), F32)],
        compiler_params=_params(("arbitrary",)),
        name="router",
    )(h, norm_g.reshape(1, d), w, b)


def _dispatch_plan(route, cnt, tm, n_tiles):
    cnt = cnt[0].astype(jnp.int32)
    padded = ((cnt + tm - 1) // tm) * tm
    ends = jnp.cumsum(padded)
    off = ends - padded
    onehot = (route[:, 0:TOP_K, None] == jnp.arange(N_EXPERTS, dtype=F32)).astype(F32)
    pos = jnp.einsum("rke,e->rk", onehot, off.astype(F32), precision=HI) + route[:, TOP_K:2 * TOP_K]
    pos = pos.astype(jnp.int32)
    pos0, pos1 = pos[:, 0], pos[:, 1]
    n_used = ends[-1] // tm
    tiles = jnp.arange(n_tiles, dtype=jnp.int32)
    tile_e = jnp.sum((tiles[:, None] * tm >= ends[None, :]).astype(jnp.int32), axis=1)
    tile_e = jnp.minimum(jnp.where(tiles < n_used, tile_e, tile_e[n_used - 1]), N_EXPERTS - 1)
    experts = jnp.arange(N_EXPERTS, dtype=jnp.int32)
    has_rows = cnt > 0
    slot_tab = (jnp.cumsum(has_rows.astype(jnp.int32)) - 1) % 2
    later = jnp.where(has_rows[None, :] & (experts[None, :] > experts[:, None]), experts[None, :], N_EXPERTS)
    next_tab = jnp.min(later, axis=1)
    next_tab = jnp.where(next_tab == N_EXPERTS, -1, next_tab)
    sel = (tile_e[:, None] == experts[None, :]).astype(jnp.int32)
    plan = dict(tile_e=tile_e, slot=jnp.sum(sel * slot_tab[None, :], axis=1), next_e=jnp.sum(sel * next_tab[None, :], axis=1),
                n_used=n_used.reshape(1))
    return pos0, pos1, plan


def _dispatch_kernel(pos0_ref, pos1_ref, u_ref, xs_init_ref, xs_ref, sem, *, t_rows):
    del xs_init_ref
    base = pl.program_id(0) * t_rows

    def body(r, c):
        src = u_ref.at[pl.ds(r, 1)]
        pltpu.make_async_copy(src, xs_ref.at[pl.ds(pos0_ref[base + r], 1)], sem).start()
        pltpu.make_async_copy(src, xs_ref.at[pl.ds(pos1_ref[base + r], 1)], sem).start()
        return c

    lax.fori_loop(0, t_rows, body, 0, unroll=DMA_UNROLL)
    for _ in range(2):
        pltpu.make_async_copy(u_ref, xs_ref.at[pl.ds(0, t_rows)], sem).wait()


def _dispatch(u, pos0, pos1, n_slots):
    r, d = u.shape
    t_rows = _divisor_tile(r, DISPATCH_T, SUBLANES)
    kern = functools.partial(_dispatch_kernel, t_rows=t_rows)
    return pl.pallas_call(
        kern,
        grid_spec=pltpu.PrefetchScalarGridSpec(
            num_scalar_prefetch=2,
            grid=(r // t_rows,),
            in_specs=[
                pl.BlockSpec((t_rows, d), lambda i, p0, p1: (i, 0)),
                pl.BlockSpec(memory_space=pl.ANY),
            ],
            out_specs=pl.BlockSpec(memory_space=pl.ANY),
            scratch_shapes=[pltpu.SemaphoreType.DMA(())],
        ),
        out_shape=jax.ShapeDtypeStruct((n_slots, d), u.dtype),
        input_output_aliases={3: 0},
        compiler_params=_params(("arbitrary",)),
        name="moe_dispatch",
    )(pos0, pos1, u, jnp.zeros((n_slots, d), u.dtype))


def _expert_kernel(te_ref, slot_ref, nxt_ref, nu_ref, x_ref, wg_hbm, wu_hbm, wd_hbm, y_ref,
                   wg_f, wu_f, wd_f, wg_b, wu_b, wd_b, sems, *, layer):
    i = pl.program_id(0)
    e = te_ref[i]
    slot = slot_ref[i]
    used = i < nu_ref[0]
    first_of_expert = used & ((i == 0) | (te_ref[jnp.maximum(i - 1, 0)] != e))

    def weight_copies(expert, s):
        return [pltpu.make_async_copy(src.at[layer, expert], dst.at[s], sems.at[s])
                for src, dst in ((wg_hbm, wg_f), (wu_hbm, wu_f), (wd_hbm, wd_f))]

    @pl.when(i == 0)
    def _():
        for c in weight_copies(e, slot):
            c.start()

    @pl.when(first_of_expert)
    def _():
        for c in weight_copies(e, slot):
            c.wait()
        nxt = nxt_ref[i]

        @pl.when(nxt >= 0)
        def _():
            for c in weight_copies(nxt, 1 - slot):
                c.start()

        wg_b[...] = wg_f[slot].astype(BF16)
        wu_b[...] = wu_f[slot].astype(BF16)
        wd_b[...] = wd_f[slot].astype(BF16)

    @pl.when(used)
    def _():
        xa, xb = _unpack_bf16_pairs(x_ref[...])
        half = xa.shape[1]
        hg = _dot(xa, wg_b[0:half, :]) + _dot(xb, wg_b[half:2 * half, :])
        hu = _dot(xa, wu_b[0:half, :]) + _dot(xb, wu_b[half:2 * half, :])
        hid = (_silu(hg) * hu).astype(BF16)
        y_ref[...] = _dot(hid, wd_b[...])

    @pl.when(jnp.logical_not(used))
    def _():
        y_ref[...] = jnp.zeros(y_ref.shape, F32)


def _experts(xs, plan, w_gate, w_up, w_down, layer, tm):
    n_slots = xs.shape[0]
    _, _, d, f = w_gate.shape
    assert xs.shape[1] * 2 == d
    n_tiles = n_slots // tm
    used_row = lambda i, te, sl, nx, nu: (jnp.minimum(i, nu[0] - 1), 0)
    hbm = pl.BlockSpec(memory_space=pl.ANY)
    kern = functools.partial(_expert_kernel, layer=layer)
    return pl.pallas_call(
        kern,
        grid_spec=pltpu.PrefetchScalarGridSpec(
            num_scalar_prefetch=4,
            grid=(n_tiles,),
            in_specs=[pl.BlockSpec((tm, d // 2), used_row), hbm, hbm, hbm],
            out_specs=pl.BlockSpec((tm, d), lambda i, te, sl, nx, nu: (i, 0)),
            scratch_shapes=[
                pltpu.VMEM((2, d, f), F32), pltpu.VMEM((2, d, f), F32), pltpu.VMEM((2, f, d), F32),
                pltpu.VMEM((d, f), BF16), pltpu.VMEM((d, f), BF16), pltpu.VMEM((f, d), BF16),
                pltpu.SemaphoreType.DMA((2,)),
            ],
        ),
        out_shape=jax.ShapeDtypeStruct((n_slots, d), F32),
        compiler_params=_params(("arbitrary",)),
        name="moe_experts",
    )(plan["tile_e"], plan["slot"], plan["next_e"], plan["n_used"], xs, w_gate, w_up, w_down)


def _combine_kernel(pos0_ref, pos1_ref, h_ref, route_ref, ys_ref, o_ref, buf0, buf1, sem, *, t_rows):
    base = pl.program_id(0) * t_rows

    def body(r, c):
        pltpu.make_async_copy(ys_ref.at[pl.ds(pos0_ref[base + r], 1)], buf0.at[pl.ds(r, 1)], sem).start()
        pltpu.make_async_copy(ys_ref.at[pl.ds(pos1_ref[base + r], 1)], buf1.at[pl.ds(r, 1)], sem).start()
        return c

    lax.fori_loop(0, t_rows, body, 0, unroll=DMA_UNROLL)
    for buf in (buf0, buf1):
        pltpu.make_async_copy(ys_ref.at[pl.ds(0, t_rows)], buf, sem).wait()
    g = route_ref[...]
    o_ref[...] = h_ref[...] + g[:, 4:5] * buf0[...] + g[:, 5:6] * buf1[...]


def _combine(h, route, ys, pos0, pos1):
    r, d = h.shape
    t_rows = _divisor_tile(r, COMBINE_T, SUBLANES)
    kern = functools.partial(_combine_kernel, t_rows=t_rows)
    return pl.pallas_call(
        kern,
        grid_spec=pltpu.PrefetchScalarGridSpec(
            num_scalar_prefetch=2,
            grid=(r // t_rows,),
            in_specs=[
                pl.BlockSpec((t_rows, d), lambda i, p0, p1: (i, 0)),
                pl.BlockSpec((t_rows, ROUTE_COLS), lambda i, p0, p1: (i, 0)),
                pl.BlockSpec(memory_space=pl.ANY),
            ],
            out_specs=pl.BlockSpec((t_rows, d), lambda i, p0, p1: (i, 0)),
            scratch_shapes=[pltpu.VMEM((t_rows, d), F32), pltpu.VMEM((t_rows, d), F32), pltpu.SemaphoreType.DMA(())],
        ),
        out_shape=jax.ShapeDtypeStruct((r, d), F32),
        compiler_params=_params(("arbitrary",)),
        name="moe_combine",
    )(pos0, pos1, h, route, ys)


def _hier_moe(h, norm_g, w_group, b_group, w_router, b_router, w_gate, w_up, w_down, layer):
    r = h.shape[0]
    tm = EXPERT_TM
    n_tiles = -(-(TOP_K * r + N_EXPERTS * (tm - 1)) // tm)
    u, route, cnt = _router(h, norm_g, w_group, b_group, w_router, b_router)
    pos0, pos1, plan = _dispatch_plan(route, cnt, tm, n_tiles)
    xs = _dispatch(u, pos0, pos1, n_tiles * tm)
    ys = _experts(xs, plan, w_gate, w_up, w_down, layer, tm)
    return _combine(h, route, ys, pos0, pos1)


def _final_kernel(h_ref, g_ref, o_ref):
    x = h_ref[...]
    ms = jnp.mean(x * x, axis=-1, keepdims=True)
    o_ref[0] = x * lax.rsqrt(ms + EPS) * g_ref[...]


def _final_norm(h, norm_g, nb, lp, seq, skip):
    d = h.shape[1]
    t_rows = _divisor_tile(seq, FINAL_T, SUBLANES)
    assert skip % SUBLANES == 0 and lp % SUBLANES == 0
    first_row = lambda b, t: (pl.multiple_of(b * lp + skip + t * t_rows, SUBLANES), 0)
    return pl.pallas_call(
        _final_kernel,
        grid=(nb, seq // t_rows),
        in_specs=[
            pl.BlockSpec((pl.Element(t_rows), pl.Element(d)), first_row),
            pl.BlockSpec((1, d), lambda b, t: (0, 0)),
        ],
        out_specs=pl.BlockSpec((1, t_rows, d), lambda b, t: (b, t, 0)),
        out_shape=jax.ShapeDtypeStruct((nb, seq, d), F32),
        compiler_params=_params(("arbitrary", "arbitrary")),
        name="final_norm",
    )(h, norm_g.reshape(1, d))


def kernel(x, meta, attn_norm, w_in, conv_dw_w, conv_dw_b, conv_ln_g, conv_ln_b, short_conv_w, a_log, dt_bias,
           delta_norm_g, w_out, ffn_norm, w_group, b_group, w_router, b_router, w_gate, w_up, w_down, final_norm):
    nb, seq, d = x.shape
    depth = w_in.shape[0]
    conv_w = conv_dw_w.shape[2]
    delta_w = short_conv_w.shape[2] // 3
    nh = delta_w // HEAD_DIM
    n_main = 2 * conv_w + 4 * delta_w
    assert w_in.shape[2] == n_main + 2 * nh and conv_w == delta_w
    ln = N_META + seq
    pad = (-ln) % CHUNK
    lp = ln + pad
    skip = pad + N_META

    meta_b = jnp.broadcast_to(meta[None].astype(x.dtype), (nb, N_META, d))
    h = jnp.concatenate([jnp.zeros((nb, pad, d), x.dtype), meta_b, x], axis=1).reshape(nb * lp, d)

    w_in_t = jnp.swapaxes(w_in, 1, 2)
    for l in range(depth):
        p, bd_col, bd_row = _inproj(h, attn_norm[l], w_in_t, l, n_main, lp, pad, nb)
        bd_row3 = bd_row.reshape(2 * nh, nb * lp // CHUNK, CHUNK).transpose(1, 0, 2)
        y_conv = _conformer_conv(p, conv_dw_w[l], conv_dw_b[l], conv_ln_g[l], conv_ln_b[l], nb, lp, conv_w)
        y_delta = _gated_deltanet(p, bd_col, bd_row3, short_conv_w[l], a_log[l], dt_bias[l], delta_norm_g[l],
                                  nb, lp, pad, delta_w, 2 * conv_w)
        h = _outproj(y_conv, y_delta, w_out, l, h)
        h = _hier_moe(h, ffn_norm[l], w_group[l], b_group[l], w_router[l], b_router[l], w_gate, w_up, w_down, l)
    return _final_norm(h, final_norm, nb, lp, seq, skip)
```

```python
import functools

import jax
import jax.numpy as jnp
from jax import lax
from jax.experimental import pallas as pl
from jax.experimental.pallas import tpu as pltpu

F32 = jnp.float32
BF16 = jnp.bfloat16
HI = lax.Precision.HIGHEST

EPS = 1e-6
CHUNK = 64
N_META = 16
CONV_GROUPS = 8
CONV_KERNEL = 31
HEAD_DIM = 128
SHORT_CONV = 4
N_GROUPS = 4
EXPERTS_PER_GROUP = 8
N_EXPERTS = N_GROUPS * EXPERTS_PER_GROUP
TOP_K = 2
ROUTE_COLS = 8
ROUTE_LO_LANE = 64
LANES = 128
SUBLANES = 8
CONV_HIST = 32
SC_HIST = 8
VMEM_LIMIT = 56 * 1024 * 1024

INPROJ_TM, INPROJ_TN = 1664, 512
CONV_CHUNKS = 13
DELTA_CHUNKS = 5
DELTA_INTERLEAVE = 3
WOUT_STAGE_ROWS = 512
ROUTER_TM = 640
EXPERT_TM = 256
DISPATCH_T = 640
COMBINE_T = 640
DMA_UNROLL = 8
FINAL_T = 512


def _divisor_tile(n, cap, mult):
    best = None
    for t in range(mult, min(n, cap) + 1, mult):
        if n % t == 0:
            best = t
    if best is None:
        raise ValueError(f"no tile for n={n} cap={cap} mult={mult}")
    return best


def _params(sem):
    return pltpu.CompilerParams(dimension_semantics=sem, vmem_limit_bytes=VMEM_LIMIT)


def _dot(a, b, precision=None):
    return jnp.dot(a, b, preferred_element_type=F32, precision=precision)


def _dot_nt(a, b, precision=None):
    return lax.dot_general(a, b, (((1,), (1,)), ((), ())), preferred_element_type=F32, precision=precision)


def _silu(x):
    return x * jax.nn.sigmoid(x)


def _softplus(x):
    return jnp.maximum(x, 0.0) + jnp.log1p(jnp.exp(-jnp.abs(x)))


def _dot_split(a, b, n_parts, split_lhs):
    x = a if split_lhs else b
    acc = None
    for _ in range(n_parts):
        piece = x.astype(BF16)
        term = _dot(piece, b) if split_lhs else _dot(a, piece)
        acc = term if acc is None else acc + term
        x = x - piece.astype(F32)
    return acc


def _pack_bf16_pairs(x):
    n = x.shape[1] // 2
    lo = lax.bitcast_convert_type(x[:, 0:n].astype(F32), jnp.uint32)
    hi = lax.bitcast_convert_type(x[:, n:2 * n].astype(F32), jnp.uint32)
    return (lo >> 16) | hi


def _unpack_bf16_pairs(w):
    lo = lax.bitcast_convert_type(w << 16, F32).astype(BF16)
    hi = lax.bitcast_convert_type(w & jnp.uint32(0xFFFF0000), F32).astype(BF16)
    return lo, hi


def _causal_taps(win, tap_w, n_taps, first_tap):
    rows = win.shape[0]
    acc = jnp.zeros((CHUNK,) + win.shape[1:], F32)
    for res in range(SUBLANES):
        offs = [o for o in range(first_tap, first_tap + n_taps) if o % SUBLANES == res]
        if not offs:
            continue
        shifted = pltpu.roll(win, rows - res, axis=0) if res else win
        for o in offs:
            assert o + CHUNK <= rows
            a = o - res
            acc = acc + tap_w(o - first_tap) * shifted[a:a + CHUNK, :]
    return acc


def _inproj_kernel(h_ref, g_ref, w_ref, wbd_ref, p_ref, bdc_ref, bdr_ref, u_sc, *, tm, rb, lp, pad, nb):
    i = pl.program_id(0)
    j = pl.program_id(1)

    @pl.when(j == 0)
    def _():
        wbd = wbd_ref[...].astype(BF16)
        for blk in range(tm // rb):
            rows = slice(blk * rb, (blk + 1) * rb)
            x = h_ref[rows, :]
            ms = jnp.mean(x * x, axis=-1, keepdims=True)
            u = x * lax.rsqrt(ms + EPS) * g_ref[...]
            row = i * tm + blk * rb + lax.broadcasted_iota(jnp.int32, (rb, 1), 0)
            valid = (row >= pad) & (row < lp)
            for b in range(1, nb):
                valid = valid | ((row >= b * lp + pad) & (row < (b + 1) * lp))
            ub = jnp.where(valid, u, 0.0).astype(BF16)
            u_sc[rows, :] = ub
            bdc_ref[rows, :] = _dot_nt(ub, wbd)
            bdr_ref[:, rows] = _dot_nt(wbd, ub)

    p_ref[...] = _dot_nt(u_sc[...], w_ref[...].astype(BF16))


def _inproj(h, norm_g, w_in_t, layer, n_main, lp, pad, nb):
    r, d = h.shape
    n_bd = w_in_t.shape[1] - n_main
    tm = _divisor_tile(r, INPROJ_TM, LANES)
    tn = _divisor_tile(n_main, INPROJ_TN, LANES)
    assert n_main % n_bd == 0 and n_bd % SUBLANES == 0
    kern = functools.partial(_inproj_kernel, tm=tm, rb=LANES, lp=lp, pad=pad, nb=nb)
    return pl.pallas_call(
        kern,
        grid=(r // tm, n_main // tn),
        in_specs=[
            pl.BlockSpec((tm, d), lambda i, j: (i, 0)),
            pl.BlockSpec((1, d), lambda i, j: (0, 0)),
            pl.BlockSpec((None, tn, d), lambda i, j: (layer, j, 0)),
            pl.BlockSpec((None, n_bd, d), lambda i, j: (layer, n_main // n_bd, 0)),
        ],
        out_specs=[
            pl.BlockSpec((tm, tn), lambda i, j: (i, j)),
            pl.BlockSpec((tm, n_bd), lambda i, j: (i, 0)),
            pl.BlockSpec((n_bd, tm), lambda i, j: (0, i)),
        ],
        out_shape=[
            jax.ShapeDtypeStruct((r, n_main), F32),
            jax.ShapeDtypeStruct((r, n_bd), F32),
            jax.ShapeDtypeStruct((n_bd, r), F32),
        ],
        scratch_shapes=[pltpu.VMEM((tm, d), BF16)],
        compiler_params=_params(("arbitrary", "arbitrary")),
        name="inproj",
    )(h, norm_g.reshape(1, d), w_in_t, w_in_t)


def _conv_kernel(a_ref, b_ref, w_ref, bias_ref, lg_ref, lb_ref, o_ref, ybuf, *, t_rows, width):
    t = pl.program_id(1)
    n_blk = t_rows // CHUNK

    @pl.when(t == 0)
    def _():
        ybuf[0:CONV_HIST, :] = jnp.zeros((CONV_HIST, width), F32)

    @pl.when(t > 0)
    def _():
        ybuf[0:CONV_HIST, :] = ybuf[t_rows:t_rows + CONV_HIST, :]

    def glu_body(r, c):
        r0 = pl.multiple_of(r * CHUNK, CHUNK)
        a = a_ref[pl.ds(r0, CHUNK), :]
        g = b_ref[pl.ds(r0, CHUNK), :]
        ybuf[pl.ds(CONV_HIST + r0, CHUNK), :] = a * jax.nn.sigmoid(g)
        return c

    lax.fori_loop(0, n_blk, glu_body, 0)

    first_tap = CONV_HIST - (CONV_KERNEL - 1)

    def body(r, c):
        r0 = pl.multiple_of(r * CHUNK, CHUNK)
        for gi in range(width // LANES):
            ls = slice(gi * LANES, (gi + 1) * LANES)
            win = ybuf[pl.ds(r0, CHUNK + CONV_HIST), ls]
            acc = _causal_taps(win, lambda k: w_ref[k:k + 1, ls], CONV_KERNEL, first_tap)
            y = acc + bias_ref[:, ls]
            mu = jnp.mean(y, axis=-1, keepdims=True)
            dlt = y - mu
            var = jnp.mean(dlt * dlt, axis=-1, keepdims=True)
            yn = dlt * lax.rsqrt(var + EPS) * lg_ref[:, ls] + lb_ref[:, ls]
            o_ref[pl.ds(r0, CHUNK), ls] = _silu(yn).astype(BF16)
        return c

    lax.fori_loop(0, n_blk, body, 0)


def _conformer_conv(p, w_dw, b_dw, ln_g, ln_b, nb, lp, width):
    r = p.shape[0]
    nch = lp // CHUNK
    t_rows = CHUNK * _divisor_tile(nch, CONV_CHUNKS, 1)
    nt = lp // t_rows
    assert width // LANES == CONV_GROUPS
    kern = functools.partial(_conv_kernel, t_rows=t_rows, width=width)
    vec = lambda b, t: (0, 0)
    return pl.pallas_call(
        kern,
        grid=(nb, nt),
        in_specs=[
            pl.BlockSpec((t_rows, width), lambda b, t: (b * nt + t, 0)),
            pl.BlockSpec((t_rows, width), lambda b, t: (b * nt + t, 1)),
            pl.BlockSpec((CONV_KERNEL, width), vec),
            pl.BlockSpec((1, width), vec),
            pl.BlockSpec((1, width), vec),
            pl.BlockSpec((1, width), vec),
        ],
        out_specs=pl.BlockSpec((t_rows, width), lambda b, t: (b * nt + t, 0)),
        out_shape=jax.ShapeDtypeStruct((r, width), BF16),
        scratch_shapes=[pltpu.VMEM((t_rows + CONV_HIST, width), F32)],
        compiler_params=_params(("arbitrary", "arbitrary")),
        name="conformer_conv",
    )(p, p, w_dw, b_dw.reshape(1, width), ln_g.reshape(1, width), ln_b.reshape(1, width))


def _delta_kernel(q_ref, k_ref, v_ref, z_ref, bdc_ref, bdr_ref, scw_ref, alr_ref, dtr_ref, alc_ref, dtc_ref,
                  ng_ref, o_ref, qbuf, kbuf, vbuf, s_sc, *, t_rows, pad, width, nh):
    t = pl.program_id(1)
    n_blk = t_rows // CHUNK
    bufs = (qbuf, kbuf, vbuf)
    srcs = (q_ref, k_ref, v_ref)

    @pl.when(t == 0)
    def _():
        s_sc[...] = jnp.zeros(s_sc.shape, F32)
        for buf in bufs:
            buf[0:SC_HIST, :] = jnp.zeros((SC_HIST, width), F32)

    @pl.when(t > 0)
    def _():
        for buf in bufs:
            buf[0:SC_HIST, :] = buf[t_rows:t_rows + SC_HIST, :]

    def copy_body(r, c):
        r0 = pl.multiple_of(r * CHUNK, CHUNK)
        for buf, src in zip(bufs, srcs):
            buf[pl.ds(SC_HIST + r0, CHUNK), :] = src[pl.ds(r0, CHUNK), :]
        return c

    lax.fori_loop(0, n_blk, copy_body, 0)

    ri = lax.broadcasted_iota(jnp.int32, (CHUNK, CHUNK), 0)
    ci = lax.broadcasted_iota(jnp.int32, (CHUNK, CHUNK), 1)
    incl = ri >= ci
    strict = ri > ci
    tril = incl.astype(BF16)
    triu = (ri <= ci).astype(BF16)
    eye = (ri == ci).astype(F32)

    def head_expand(lanes):
        eh = lax.broadcasted_iota(jnp.int32, (nh, nh * lanes), 0)
        ec = lax.broadcasted_iota(jnp.int32, (nh, nh * lanes), 1)
        return ((ec >= eh * lanes) & (ec < (eh + 1) * lanes)).astype(BF16)

    expand_c = head_expand(CHUNK)
    expand_d = head_expand(HEAD_DIM)
    first_tap = SC_HIST - (SHORT_CONV - 1)
    q_scale = HEAD_DIM ** -0.5
    heads = range(nh)

    def short_conv(buf, part, r0):
        win = buf[pl.ds(r0, CHUNK + SC_HIST), :]
        acc = _causal_taps(win, lambda k: scw_ref[k:k + 1, part * width:(part + 1) * width], SHORT_CONV, first_tap)
        return _silu(acc)

    hd = [slice(h * HEAD_DIM, (h + 1) * HEAD_DIM) for h in heads]
    hc = [slice(h * CHUNK, (h + 1) * CHUNK) for h in heads]

    def chunk_inputs(c):
        r0 = pl.multiple_of(c * CHUNK, CHUNK)
        lrow = t * t_rows + r0 + lax.broadcasted_iota(jnp.int32, (CHUNK, 1), 0)
        lcol = t * t_rows + r0 + lax.broadcasted_iota(jnp.int32, (1, CHUNK), 1)
        bl = bdc_ref[pl.ds(r0, CHUNK), :]
        br = bdr_ref[c]
        beta_col = jnp.where(lrow >= pad, jax.nn.sigmoid(bl[:, 0:nh]), 0.0)
        beta_row = jnp.where(lcol >= pad, jax.nn.sigmoid(br[0:nh, :]), 0.0)
        g_col = jnp.where(lrow >= pad, -jnp.exp(alr_ref[...]) * _softplus(bl[:, nh:2 * nh] + dtr_ref[...]), 0.0)
        g_row = jnp.where(lcol >= pad, -jnp.exp(alc_ref[...]) * _softplus(br[nh:2 * nh, :] + dtc_ref[...]), 0.0)
        gc_col = _dot_split(tril, g_col, 3, split_lhs=False)
        gc_row = _dot_split(g_row, triu, 3, split_lhs=True)
        gc_x = _dot_split(gc_col, expand_c, 3, split_lhs=True)
        qc = short_conv(qbuf, 0, r0)
        kc = short_conv(kbuf, 1, r0)
        vc = short_conv(vbuf, 2, r0)
        qn = [qc[:, s] * lax.rsqrt(jnp.sum(qc[:, s] * qc[:, s], axis=-1, keepdims=True) + EPS) * q_scale for s in hd]
        kn = [kc[:, s] * lax.rsqrt(jnp.sum(kc[:, s] * kc[:, s], axis=-1, keepdims=True) + EPS) for s in hd]
        return dict(
            r0=r0, beta_row=beta_row, eg_row=jnp.exp(gc_row), ekd_row=jnp.exp(gc_row[:, CHUNK - 1:CHUNK] - gc_row),
            beta_x=_dot_split(beta_col, expand_c, 2, split_lhs=True),
            eg_x=_dot_split(jnp.exp(gc_col), expand_d, 2, split_lhs=True),
            qn=qn, kn=kn, k16=[k.astype(BF16) for k in kn], v16=[vc[:, s].astype(BF16) for s in hd],
            decay=[jnp.where(incl, jnp.exp(jnp.where(incl, gc_x[:, hc[h]] - gc_row[h:h + 1, :], 0.0)), 0.0) for h in heads],
            zc=z_ref[pl.ds(r0, CHUNK), :])

    def process(chunks):
        cin = [chunk_inputs(c) for c in chunks]
        pairs = [(ci, h) for ci in range(len(chunks)) for h in heads]
        kq = [_dot_nt(jnp.concatenate([cin[ci]["k16"][h], cin[ci]["qn"][h].astype(BF16)], axis=0), cin[ci]["k16"][h])
              for ci, h in pairs]
        xp = [jnp.where(strict, -(kq[n][0:CHUNK] * cin[ci]["beta_x"][:, hc[h]] * cin[ci]["decay"][h]), 0.0)
              for n, (ci, h) in enumerate(pairs)]
        intra = [kq[n][CHUNK:2 * CHUNK] * cin[ci]["decay"][h] for n, (ci, h) in enumerate(pairs)]
        ainv = [eye + x for x in xp]
        n_sq = 1
        while 2 * n_sq < CHUNK:
            xp16 = [x.astype(BF16) for x in xp]
            xp = [_dot(x, x) for x in xp16]
            ainv = [a + _dot(a.astype(BF16), x.astype(BF16)) for a, x in zip(ainv, xp)]
            n_sq *= 2
        u = [_dot((ainv[n] * cin[ci]["beta_row"][h:h + 1, :]).astype(BF16), cin[ci]["v16"][h]) for n, (ci, h) in enumerate(pairs)]
        w = [_dot((ainv[n] * (cin[ci]["beta_row"][h:h + 1, :] * cin[ci]["eg_row"][h:h + 1, :])).astype(BF16), cin[ci]["k16"][h])
             for n, (ci, h) in enumerate(pairs)]
        for ci, cc in enumerate(cin):
            base = ci * nh
            q_dec = [cc["qn"][h] * cc["eg_x"][:, hd[h]] for h in heads]
            kd_t = [cc["kn"][h].T * cc["ekd_row"][h:h + 1, :] for h in heads]
            s_old = [s_sc[h] for h in heads]
            wq_s = [_dot(jnp.concatenate([w[base + h], q_dec[h]], axis=0).astype(BF16), s_old[h].astype(BF16)) for h in heads]
            v_new = [u[base + h] - wq_s[h][0:CHUNK] for h in heads]
            iv = [_dot(jnp.concatenate([intra[base + h], kd_t[h]], axis=0).astype(BF16), v_new[h].astype(BF16)) for h in heads]
            for h in heads:
                o = wq_s[h][CHUNK:2 * CHUNK] + iv[h][0:CHUNK]
                s_sc[h] = s_old[h] * cc["eg_x"][CHUNK - 1:CHUNK, hd[h]] + iv[h][CHUNK:CHUNK + HEAD_DIM]
                on = o * lax.rsqrt(jnp.mean(o * o, axis=-1, keepdims=True) + EPS) * ng_ref[...]
                o_ref[pl.ds(cc["r0"], CHUNK), hd[h]] = (on * _silu(cc["zc"][:, hd[h]])).astype(BF16)

    n_grp = n_blk // DELTA_INTERLEAVE

    def body(g, carry):
        process([g * DELTA_INTERLEAVE + k for k in range(DELTA_INTERLEAVE)])
        return carry

    lax.fori_loop(0, n_grp, body, 0)
    if n_blk % DELTA_INTERLEAVE:
        process(list(range(n_grp * DELTA_INTERLEAVE, n_blk)))


def _gated_deltanet(p, bd_col, bd_row3, sc_w, a_log, dt_bias, norm_g, nb, lp, pad, width, col0):
    r = p.shape[0]
    nh = width // HEAD_DIM
    nch = lp // CHUNK
    cpt = _divisor_tile(nch, DELTA_CHUNKS, 1)
    t_rows = CHUNK * cpt
    nt = lp // t_rows
    cb = col0 // width
    assert col0 % width == 0
    kern = functools.partial(_delta_kernel, t_rows=t_rows, pad=pad, width=width, nh=nh)
    vec = lambda b, t: (0, 0)
    part = lambda off: pl.BlockSpec((t_rows, width), lambda b, t: (b * nt + t, cb + off))
    return pl.pallas_call(
        kern,
        grid=(nb, nt),
        in_specs=[
            part(0), part(1), part(2), part(3),
            pl.BlockSpec((t_rows, 2 * nh), lambda b, t: (b * nt + t, 0)),
            pl.BlockSpec((cpt, 2 * nh, CHUNK), lambda b, t: (b * nt + t, 0, 0)),
            pl.BlockSpec((SHORT_CONV, 3 * width), vec),
            pl.BlockSpec((1, nh), vec),
            pl.BlockSpec((1, nh), vec),
            pl.BlockSpec((nh, 1), vec),
            pl.BlockSpec((nh, 1), vec),
            pl.BlockSpec((1, HEAD_DIM), vec),
        ],
        out_specs=pl.BlockSpec((t_rows, width), lambda b, t: (b * nt + t, 0)),
        out_shape=jax.ShapeDtypeStruct((r, width), BF16),
        scratch_shapes=[
            pltpu.VMEM((t_rows + SC_HIST, width), F32),
            pltpu.VMEM((t_rows + SC_HIST, width), F32),
            pltpu.VMEM((t_rows + SC_HIST, width), F32),
            pltpu.VMEM((nh, HEAD_DIM, HEAD_DIM), F32),
        ],
        compiler_params=_params(("arbitrary", "arbitrary")),
        name="gated_deltanet",
    )(p, p, p, p, bd_col, bd_row3, sc_w, a_log.reshape(1, nh), dt_bias.reshape(1, nh),
      a_log.reshape(nh, 1), dt_bias.reshape(nh, 1), norm_g.reshape(1, HEAD_DIM))


def _first_argmax(vals, iota, n):
    m = jnp.max(vals, axis=-1, keepdims=True)
    idx = jnp.min(jnp.where(vals == m, iota, n), axis=-1, keepdims=True)
    return m, idx


def _outproj_router_kernel(yc_ref, yd_ref, h_ref, wout_hbm, g_ref, w_ref, b_ref, h2_ref, u_ref, route_ref, cnt_ref,
                           carry_sc, w16, stage, sems, *, layer):
    kw = yc_ref.shape[1]
    rows = stage.shape[1]
    n_chunk = w16.shape[0] // rows

    @pl.when(pl.program_id(0) == 0)
    def _():
        carry_sc[...] = jnp.zeros(carry_sc.shape, F32)

        def chunk_copy(c):
            return pltpu.make_async_copy(wout_hbm.at[layer, pl.ds(c * rows, rows)], stage.at[c % 2], sems.at[c % 2])

        chunk_copy(0).start()
        for c in range(n_chunk):
            if c + 1 < n_chunk:
                chunk_copy(c + 1).start()
            chunk_copy(c).wait()
            w16[c * rows:(c + 1) * rows, :] = stage[c % 2].astype(BF16)

    x = h_ref[...] + (_dot(yc_ref[...], w16[0:kw, :]) + _dot(yd_ref[...], w16[kw:2 * kw, :]))
    h2_ref[...] = x
    ms = jnp.mean(x * x, axis=-1, keepdims=True)
    u = x * lax.rsqrt(ms + EPS) * g_ref[...]
    w = w_ref[...]
    nl = b_ref.shape[1]
    uh = u.astype(BF16)
    ul = (u - uh.astype(F32)).astype(BF16)
    u_ref[...] = _pack_bf16_pairs(uh)
    w_hi = w.astype(BF16).astype(F32)
    lane = lax.broadcasted_iota(jnp.int32, w.shape, 1)
    w_hl = jnp.where(lane < ROUTE_LO_LANE, w_hi, w - w_hi).astype(BF16)
    by_hi = _dot(uh, w_hl)
    by_lo = _dot(ul, w_hl)
    logits = by_hi[:, 0:nl] + (by_hi[:, ROUTE_LO_LANE:ROUTE_LO_LANE + nl] + by_lo[:, 0:nl]) + b_ref[...]
    tm = x.shape[0]
    glog = logits[:, 0:N_GROUPS]
    elog = logits[:, N_GROUPS:N_GROUPS + N_EXPERTS]
    gi = lax.broadcasted_iota(jnp.int32, (tm, N_GROUPS), 1)
    gmax, gsel = _first_argmax(glog, gi, N_GROUPS)
    p_group = 1.0 / jnp.sum(jnp.exp(glog - gmax), axis=-1, keepdims=True)
    ei = lax.broadcasted_iota(jnp.int32, (tm, N_EXPERTS), 1)
    in_group = (ei >= gsel * EXPERTS_PER_GROUP) & (ei < (gsel + 1) * EXPERTS_PER_GROUP)
    neg = jnp.float32(-jnp.inf)
    cand = jnp.where(in_group, elog, neg)
    m1, i1 = _first_argmax(cand, ei, N_EXPERTS)
    cand2 = jnp.where(ei == i1, neg, cand)
    m2, i2 = _first_argmax(cand2, ei, N_EXPERTS)
    e2 = jnp.exp(m2 - m1)
    w1 = p_group / (1.0 + e2)
    w2 = p_group * e2 / (1.0 + e2)
    oh1 = (ei == i1).astype(F32)
    oh2 = (ei == i2).astype(F32)
    oh = oh1 + oh2
    ri = lax.broadcasted_iota(jnp.int32, (tm, tm), 0)
    ci = lax.broadcasted_iota(jnp.int32, (tm, tm), 1)
    before = _dot((ri > ci).astype(BF16), oh.astype(BF16)) + carry_sc[...]
    r1 = jnp.sum(before * oh1, axis=-1, keepdims=True)
    r2 = jnp.sum(before * oh2, axis=-1, keepdims=True)
    carry_sc[...] += jnp.sum(oh, axis=0, keepdims=True)
    cnt_ref[...] = carry_sc[...]
    li = lax.broadcasted_iota(jnp.int32, (tm, ROUTE_COLS), 1)
    rec = jnp.zeros((tm, ROUTE_COLS), F32)
    for k, col in enumerate((i1.astype(F32), i2.astype(F32), r1, r2, w1, w2)):
        rec = jnp.where(li == k, col, rec)
    route_ref[...] = rec


def _outproj_router(y_conv, y_delta, w_out, layer, h, norm_g, w_group, b_group, w_router, b_router):
    r, d = h.shape
    kw = y_conv.shape[1]
    tm = _divisor_tile(r, ROUTER_TM, 16)
    w = jnp.concatenate([w_group, w_router], axis=1)
    b = jnp.concatenate([b_group, b_router]).reshape(1, -1)
    nl = w.shape[1]
    assert nl <= ROUTE_LO_LANE
    gap = jnp.zeros((d, ROUTE_LO_LANE - nl), w.dtype)
    w = jnp.concatenate([w, gap, w, gap], axis=1)
    stage_rows = _divisor_tile(2 * kw, WOUT_STAGE_ROWS, SUBLANES)
    kern = functools.partial(_outproj_router_kernel, layer=layer)
    rowblk = lambda width: pl.BlockSpec((tm, width), lambda i: (i, 0))
    const = lambda shape: pl.BlockSpec(shape, lambda i: (0, 0))
    return pl.pallas_call(
        kern,
        grid=(r // tm,),
        in_specs=[rowblk(kw), rowblk(kw), rowblk(d), pl.BlockSpec(memory_space=pl.ANY),
                  const((1, d)), const((d, 2 * ROUTE_LO_LANE)), const((1, nl))],
        out_specs=[rowblk(d), rowblk(d // 2), rowblk(ROUTE_COLS), const((1, N_EXPERTS))],
        out_shape=[
            jax.ShapeDtypeStruct((r, d), F32),
            jax.ShapeDtypeStruct((r, d // 2), jnp.uint32),
            jax.ShapeDtypeStruct((r, ROUTE_COLS), F32),
            jax.ShapeDtypeStruct((1, N_EXPERTS), F32),
        ],
        scratch_shapes=[
            pltpu.VMEM((1, N_EXPERTS), F32),
            pltpu.VMEM((2 * kw, d), BF16),
            pltpu.VMEM((2, stage_rows, d), F32),
            pltpu.SemaphoreType.DMA((2,)),
        ],
        compiler_params=_params(("arbitrary",)),
        name="outproj_router",
    )(y_conv, y_delta, h, w_out, norm_g.reshape(1, d), w, b)


def _dispatch_plan(route, cnt, tm, n_tiles):
    cnt = cnt[0].astype(jnp.int32)
    padded = ((cnt + tm - 1) // tm) * tm
    ends = jnp.cumsum(padded)
    off = ends - padded
    onehot = (route[:, 0:TOP_K, None] == jnp.arange(N_EXPERTS, dtype=F32)).astype(F32)
    pos = jnp.einsum("rke,e->rk", onehot, off.astype(F32), precision=HI) + route[:, TOP_K:2 * TOP_K]
    pos = pos.astype(jnp.int32)
    pos0, pos1 = pos[:, 0], pos[:, 1]
    n_used = ends[-1] // tm
    tiles = jnp.arange(n_tiles, dtype=jnp.int32)
    tile_e = jnp.sum((tiles[:, None] * tm >= ends[None, :]).astype(jnp.int32), axis=1)
    tile_e = jnp.minimum(jnp.where(tiles < n_used, tile_e, tile_e[n_used - 1]), N_EXPERTS - 1)
    experts = jnp.arange(N_EXPERTS, dtype=jnp.int32)
    has_rows = cnt > 0
    slot_tab = (jnp.cumsum(has_rows.astype(jnp.int32)) - 1) % 2
    later = jnp.where(has_rows[None, :] & (experts[None, :] > experts[:, None]), experts[None, :], N_EXPERTS)
    next_tab = jnp.min(later, axis=1)
    next_tab = jnp.where(next_tab == N_EXPERTS, -1, next_tab)
    sel = (tile_e[:, None] == experts[None, :]).astype(jnp.int32)
    plan = dict(tile_e=tile_e, slot=jnp.sum(sel * slot_tab[None, :], axis=1), next_e=jnp.sum(sel * next_tab[None, :], axis=1),
                n_used=n_used.reshape(1))
    return pos0, pos1, plan


def _dispatch_kernel(pos0_ref, pos1_ref, u_ref, xs_init_ref, xs_ref, sem, *, t_rows):
    del xs_init_ref
    base = pl.program_id(0) * t_rows

    def body(r, c):
        src = u_ref.at[pl.ds(r, 1)]
        pltpu.make_async_copy(src, xs_ref.at[pl.ds(pos0_ref[base + r], 1)], sem).start()
        pltpu.make_async_copy(src, xs_ref.at[pl.ds(pos1_ref[base + r], 1)], sem).start()
        return c

    lax.fori_loop(0, t_rows, body, 0, unroll=DMA_UNROLL)
    for _ in range(2):
        pltpu.make_async_copy(u_ref, xs_ref.at[pl.ds(0, t_rows)], sem).wait()


def _dispatch(u, pos0, pos1, n_slots):
    r, d = u.shape
    t_rows = _divisor_tile(r, DISPATCH_T, SUBLANES)
    kern = functools.partial(_dispatch_kernel, t_rows=t_rows)
    return pl.pallas_call(
        kern,
        grid_spec=pltpu.PrefetchScalarGridSpec(
            num_scalar_prefetch=2,
            grid=(r // t_rows,),
            in_specs=[
                pl.BlockSpec((t_rows, d), lambda i, p0, p1: (i, 0)),
                pl.BlockSpec(memory_space=pl.ANY),
            ],
            out_specs=pl.BlockSpec(memory_space=pl.ANY),
            scratch_shapes=[pltpu.SemaphoreType.DMA(())],
        ),
        out_shape=jax.ShapeDtypeStruct((n_slots, d), u.dtype),
        input_output_aliases={3: 0},
        compiler_params=_params(("arbitrary",)),
        name="moe_dispatch",
    )(pos0, pos1, u, jnp.zeros((n_slots, d), u.dtype))


def _expert_kernel(te_ref, slot_ref, nxt_ref, nu_ref, x_ref, wg_hbm, wu_hbm, wd_hbm, y_ref,
                   wg_f, wu_f, wd_f, wg_b, wu_b, wd_b, sems, *, layer):
    i = pl.program_id(0)
    e = te_ref[i]
    slot = slot_ref[i]
    used = i < nu_ref[0]
    first_of_expert = used & ((i == 0) | (te_ref[jnp.maximum(i - 1, 0)] != e))

    def weight_copies(expert, s):
        return [pltpu.make_async_copy(src.at[layer, expert], dst.at[s], sems.at[s])
                for src, dst in ((wg_hbm, wg_f), (wu_hbm, wu_f), (wd_hbm, wd_f))]

    @pl.when(i == 0)
    def _():
        for c in weight_copies(e, slot):
            c.start()

    @pl.when(first_of_expert)
    def _():
        for c in weight_copies(e, slot):
            c.wait()
        nxt = nxt_ref[i]

        @pl.when(nxt >= 0)
        def _():
            for c in weight_copies(nxt, 1 - slot):
                c.start()

        wg_b[...] = wg_f[slot].astype(BF16)
        wu_b[...] = wu_f[slot].astype(BF16)
        wd_b[...] = wd_f[slot].astype(BF16)

    @pl.when(used)
    def _():
        xa, xb = _unpack_bf16_pairs(x_ref[...])
        half = xa.shape[1]
        hg = _dot(xa, wg_b[0:half, :]) + _dot(xb, wg_b[half:2 * half, :])
        hu = _dot(xa, wu_b[0:half, :]) + _dot(xb, wu_b[half:2 * half, :])
        hid = (_silu(hg) * hu).astype(BF16)
        y_ref[...] = _dot(hid, wd_b[...])

    @pl.when(jnp.logical_not(used))
    def _():
        y_ref[...] = jnp.zeros(y_ref.shape, F32)


def _experts(xs, plan, w_gate, w_up, w_down, layer, tm):
    n_slots = xs.shape[0]
    _, _, d, f = w_gate.shape
    assert xs.shape[1] * 2 == d
    n_tiles = n_slots // tm
    used_row = lambda i, te, sl, nx, nu: (jnp.minimum(i, nu[0] - 1), 0)
    hbm = pl.BlockSpec(memory_space=pl.ANY)
    kern = functools.partial(_expert_kernel, layer=layer)
    return pl.pallas_call(
        kern,
        grid_spec=pltpu.PrefetchScalarGridSpec(
            num_scalar_prefetch=4,
            grid=(n_tiles,),
            in_specs=[pl.BlockSpec((tm, d // 2), used_row), hbm, hbm, hbm],
            out_specs=pl.BlockSpec((tm, d), lambda i, te, sl, nx, nu: (i, 0)),
            scratch_shapes=[
                pltpu.VMEM((2, d, f), F32), pltpu.VMEM((2, d, f), F32), pltpu.VMEM((2, f, d), F32),
                pltpu.VMEM((d, f), BF16), pltpu.VMEM((d, f), BF16), pltpu.VMEM((f, d), BF16),
                pltpu.SemaphoreType.DMA((2,)),
            ],
        ),
        out_shape=jax.ShapeDtypeStruct((n_slots, d), F32),
        compiler_params=_params(("arbitrary",)),
        name="moe_experts",
    )(plan["tile_e"], plan["slot"], plan["next_e"], plan["n_used"], xs, w_gate, w_up, w_down)


def _combine_kernel(pos0_ref, pos1_ref, h_ref, route_ref, ys_ref, o_ref, buf0, buf1, sems, *, t_rows, n_steps):
    i = pl.program_id(0)

    def gather_tile(step, slot):
        base = step * t_rows

        def body(r, c):
            pltpu.make_async_copy(ys_ref.at[pl.ds(pos0_ref[base + r], 1)], buf0.at[slot, pl.ds(r, 1)], sems.at[slot]).start()
            pltpu.make_async_copy(ys_ref.at[pl.ds(pos1_ref[base + r], 1)], buf1.at[slot, pl.ds(r, 1)], sems.at[slot]).start()
            return c

        lax.fori_loop(0, t_rows, body, 0, unroll=DMA_UNROLL)

    @pl.when(i == 0)
    def _():
        gather_tile(0, 0)

    @pl.when(i + 1 < n_steps)
    def _():
        gather_tile(i + 1, (i + 1) % 2)

    slot = i % 2
    for buf in (buf0, buf1):
        pltpu.make_async_copy(ys_ref.at[pl.ds(0, t_rows)], buf.at[slot], sems.at[slot]).wait()
    g = route_ref[...]
    o_ref[...] = h_ref[...] + g[:, 4:5] * buf0[slot] + g[:, 5:6] * buf1[slot]


def _combine(h, route, ys, pos0, pos1):
    r, d = h.shape
    t_rows = _divisor_tile(r, COMBINE_T, SUBLANES)
    n_steps = r // t_rows
    kern = functools.partial(_combine_kernel, t_rows=t_rows, n_steps=n_steps)
    return pl.pallas_call(
        kern,
        grid_spec=pltpu.PrefetchScalarGridSpec(
            num_scalar_prefetch=2,
            grid=(n_steps,),
            in_specs=[
                pl.BlockSpec((t_rows, d), lambda i, p0, p1: (i, 0)),
                pl.BlockSpec((t_rows, ROUTE_COLS), lambda i, p0, p1: (i, 0)),
                pl.BlockSpec(memory_space=pl.ANY),
            ],
            out_specs=pl.BlockSpec((t_rows, d), lambda i, p0, p1: (i, 0)),
            scratch_shapes=[pltpu.VMEM((2, t_rows, d), F32), pltpu.VMEM((2, t_rows, d), F32),
                            pltpu.SemaphoreType.DMA((2,))],
        ),
        out_shape=jax.ShapeDtypeStruct((r, d), F32),
        compiler_params=_params(("arbitrary",)),
        name="moe_combine",
    )(pos0, pos1, h, route, ys)


def _routed_experts(h, u, route, cnt, w_gate, w_up, w_down, layer):
    r = h.shape[0]
    tm = EXPERT_TM
    n_tiles = -(-(TOP_K * r + N_EXPERTS * (tm - 1)) // tm)
    pos0, pos1, plan = _dispatch_plan(route, cnt, tm, n_tiles)
    xs = _dispatch(u, pos0, pos1, n_tiles * tm)
    ys = _experts(xs, plan, w_gate, w_up, w_down, layer, tm)
    return _combine(h, route, ys, pos0, pos1)


def _final_kernel(h_ref, g_ref, o_ref):
    x = h_ref[...]
    ms = jnp.mean(x * x, axis=-1, keepdims=True)
    o_ref[0] = x * lax.rsqrt(ms + EPS) * g_ref[...]


def _final_norm(h, norm_g, nb, lp, seq, skip):
    d = h.shape[1]
    t_rows = _divisor_tile(seq, FINAL_T, SUBLANES)
    assert skip % SUBLANES == 0 and lp % SUBLANES == 0
    first_row = lambda b, t: (pl.multiple_of(b * lp + skip + t * t_rows, SUBLANES), 0)
    return pl.pallas_call(
        _final_kernel,
        grid=(nb, seq // t_rows),
        in_specs=[
            pl.BlockSpec((pl.Element(t_rows), pl.Element(d)), first_row),
            pl.BlockSpec((1, d), lambda b, t: (0, 0)),
        ],
        out_specs=pl.BlockSpec((1, t_rows, d), lambda b, t: (b, t, 0)),
        out_shape=jax.ShapeDtypeStruct((nb, seq, d), F32),
        compiler_params=_params(("arbitrary", "arbitrary")),
        name="final_norm",
    )(h, norm_g.reshape(1, d))


def kernel(x, meta, attn_norm, w_in, conv_dw_w, conv_dw_b, conv_ln_g, conv_ln_b, short_conv_w, a_log, dt_bias,
           delta_norm_g, w_out, ffn_norm, w_group, b_group, w_router, b_router, w_gate, w_up, w_down, final_norm):
    nb, seq, d = x.shape
    depth = w_in.shape[0]
    conv_w = conv_dw_w.shape[2]
    delta_w = short_conv_w.shape[2] // 3
    nh = delta_w // HEAD_DIM
    n_main = 2 * conv_w + 4 * delta_w
    assert w_in.shape[2] == n_main + 2 * nh and conv_w == delta_w
    ln = N_META + seq
    pad = (-ln) % CHUNK
    lp = ln + pad
    skip = pad + N_META

    meta_b = jnp.broadcast_to(meta[None].astype(x.dtype), (nb, N_META, d))
    h = jnp.concatenate([jnp.zeros((nb, pad, d), x.dtype), meta_b, x], axis=1).reshape(nb * lp, d)

    w_in_t = jnp.swapaxes(w_in, 1, 2)
    for l in range(depth):
        p, bd_col, bd_row = _inproj(h, attn_norm[l], w_in_t, l, n_main, lp, pad, nb)
        bd_row3 = bd_row.reshape(2 * nh, nb * lp // CHUNK, CHUNK).transpose(1, 0, 2)
        y_conv = _conformer_conv(p, conv_dw_w[l], conv_dw_b[l], conv_ln_g[l], conv_ln_b[l], nb, lp, conv_w)
        y_delta = _gated_deltanet(p, bd_col, bd_row3, short_conv_w[l], a_log[l], dt_bias[l], delta_norm_g[l],
                                  nb, lp, pad, delta_w, 2 * conv_w)
        h, u, route, cnt = _outproj_router(y_conv, y_delta, w_out, l, h, ffn_norm[l], w_group[l], b_group[l],
                                           w_router[l], b_router[l])
        h = _routed_experts(h, u, route, cnt, w_gate, w_up, w_down, l)
    return _final_norm(h, final_norm, nb, lp, seq, skip)
```

```python
import functools

import jax
import jax.numpy as jnp
from jax import lax
from jax.experimental import pallas as pl
from jax.experimental.pallas import tpu as pltpu

F32 = jnp.float32
BF16 = jnp.bfloat16
HI = lax.Precision.HIGHEST

EPS = 1e-6
CHUNK = 64
N_META = 16
CONV_GROUPS = 8
CONV_KERNEL = 31
HEAD_DIM = 128
SHORT_CONV = 4
N_GROUPS = 4
EXPERTS_PER_GROUP = 8
N_EXPERTS = N_GROUPS * EXPERTS_PER_GROUP
TOP_K = 2
ROUTE_COLS = 8
ROUTE_LO_LANE = 64
LANES = 128
SUBLANES = 8
CONV_HIST = 32
SC_HIST = 8
VMEM_LIMIT = 56 * 1024 * 1024

INPROJ_TM, INPROJ_TN = 1664, 512
CONV_CHUNKS = 13
DELTA_CHUNKS = 5
DELTA_INTERLEAVE = 3
WOUT_STAGE_ROWS = 512
ROUTER_TM = 640
EXPERT_TM = 256
DISPATCH_T = 640
COMBINE_T = 640
DMA_UNROLL = 8
FINAL_T = 512


def _divisor_tile(n, cap, mult):
    best = None
    for t in range(mult, min(n, cap) + 1, mult):
        if n % t == 0:
            best = t
    if best is None:
        raise ValueError(f"no tile for n={n} cap={cap} mult={mult}")
    return best


def _params(sem):
    return pltpu.CompilerParams(dimension_semantics=sem, vmem_limit_bytes=VMEM_LIMIT)


def _dot(a, b, precision=None):
    return jnp.dot(a, b, preferred_element_type=F32, precision=precision)


def _dot_nt(a, b, precision=None):
    return lax.dot_general(a, b, (((1,), (1,)), ((), ())), preferred_element_type=F32, precision=precision)


def _silu(x):
    return x * jax.nn.sigmoid(x)


def _softplus(x):
    return jnp.maximum(x, 0.0) + jnp.log1p(jnp.exp(-jnp.abs(x)))


def _dot_split(a, b, n_parts, split_lhs):
    x = a if split_lhs else b
    acc = None
    for _ in range(n_parts):
        piece = x.astype(BF16)
        term = _dot(piece, b) if split_lhs else _dot(a, piece)
        acc = term if acc is None else acc + term
        x = x - piece.astype(F32)
    return acc


def _pack_bf16_pairs(x):
    n = x.shape[1] // 2
    lo = lax.bitcast_convert_type(x[:, 0:n].astype(F32), jnp.uint32)
    hi = lax.bitcast_convert_type(x[:, n:2 * n].astype(F32), jnp.uint32)
    return (lo >> 16) | hi


def _unpack_bf16_pairs(w, dtype=BF16):
    lo = lax.bitcast_convert_type(w << 16, F32).astype(dtype)
    hi = lax.bitcast_convert_type(w & jnp.uint32(0xFFFF0000), F32).astype(dtype)
    return lo, hi


def _causal_taps(win, tap_w, n_taps, first_tap):
    rows = win.shape[0]
    acc = jnp.zeros((CHUNK,) + win.shape[1:], F32)
    for res in range(SUBLANES):
        offs = [o for o in range(first_tap, first_tap + n_taps) if o % SUBLANES == res]
        if not offs:
            continue
        shifted = pltpu.roll(win, rows - res, axis=0) if res else win
        for o in offs:
            assert o + CHUNK <= rows
            a = o - res
            acc = acc + tap_w(o - first_tap) * shifted[a:a + CHUNK, :]
    return acc


def _inproj_kernel(h_ref, g_ref, w_ref, wbd_ref, p_ref, bdc_ref, bdr_ref, u_sc, *, tm, rb, lp, pad, nb):
    i = pl.program_id(0)
    j = pl.program_id(1)

    @pl.when(j == 0)
    def _():
        wbd = wbd_ref[...].astype(BF16)
        for blk in range(tm // rb):
            rows = slice(blk * rb, (blk + 1) * rb)
            x = h_ref[rows, :]
            ms = jnp.mean(x * x, axis=-1, keepdims=True)
            u = x * lax.rsqrt(ms + EPS) * g_ref[...]
            row = i * tm + blk * rb + lax.broadcasted_iota(jnp.int32, (rb, 1), 0)
            valid = (row >= pad) & (row < lp)
            for b in range(1, nb):
                valid = valid | ((row >= b * lp + pad) & (row < (b + 1) * lp))
            ub = jnp.where(valid, u, 0.0).astype(BF16)
            u_sc[rows, :] = ub
            bdc_ref[rows, :] = _dot_nt(ub, wbd)
            bdr_ref[:, rows] = _dot_nt(wbd, ub)

    p_ref[...] = _dot_nt(u_sc[...], w_ref[...].astype(BF16))


def _inproj(h, norm_g, w_in_t, layer, n_main, lp, pad, nb):
    r, d = h.shape
    n_bd = w_in_t.shape[1] - n_main
    tm = _divisor_tile(r, INPROJ_TM, LANES)
    tn = _divisor_tile(n_main, INPROJ_TN, LANES)
    assert n_main % n_bd == 0 and n_bd % SUBLANES == 0
    kern = functools.partial(_inproj_kernel, tm=tm, rb=LANES, lp=lp, pad=pad, nb=nb)
    return pl.pallas_call(
        kern,
        grid=(r // tm, n_main // tn),
        in_specs=[
            pl.BlockSpec((tm, d), lambda i, j: (i, 0)),
            pl.BlockSpec((1, d), lambda i, j: (0, 0)),
            pl.BlockSpec((None, tn, d), lambda i, j: (layer, j, 0)),
            pl.BlockSpec((None, n_bd, d), lambda i, j: (layer, n_main // n_bd, 0)),
        ],
        out_specs=[
            pl.BlockSpec((tm, tn), lambda i, j: (i, j)),
            pl.BlockSpec((tm, n_bd), lambda i, j: (i, 0)),
            pl.BlockSpec((n_bd, tm), lambda i, j: (0, i)),
        ],
        out_shape=[
            jax.ShapeDtypeStruct((r, n_main), F32),
            jax.ShapeDtypeStruct((r, n_bd), F32),
            jax.ShapeDtypeStruct((n_bd, r), F32),
        ],
        scratch_shapes=[pltpu.VMEM((tm, d), BF16)],
        compiler_params=_params(("arbitrary", "arbitrary")),
        name="inproj",
    )(h, norm_g.reshape(1, d), w_in_t, w_in_t)


def _conv_kernel(a_ref, b_ref, w_ref, bias_ref, lg_ref, lb_ref, o_ref, ybuf, *, t_rows, width):
    t = pl.program_id(1)
    n_blk = t_rows // CHUNK

    @pl.when(t == 0)
    def _():
        ybuf[0:CONV_HIST, :] = jnp.zeros((CONV_HIST, width), F32)

    @pl.when(t > 0)
    def _():
        ybuf[0:CONV_HIST, :] = ybuf[t_rows:t_rows + CONV_HIST, :]

    def glu_body(r, c):
        r0 = pl.multiple_of(r * CHUNK, CHUNK)
        a = a_ref[pl.ds(r0, CHUNK), :]
        g = b_ref[pl.ds(r0, CHUNK), :]
        ybuf[pl.ds(CONV_HIST + r0, CHUNK), :] = a * jax.nn.sigmoid(g)
        return c

    lax.fori_loop(0, n_blk, glu_body, 0)

    first_tap = CONV_HIST - (CONV_KERNEL - 1)

    def body(r, c):
        r0 = pl.multiple_of(r * CHUNK, CHUNK)
        for gi in range(width // LANES):
            ls = slice(gi * LANES, (gi + 1) * LANES)
            win = ybuf[pl.ds(r0, CHUNK + CONV_HIST), ls]
            acc = _causal_taps(win, lambda k: w_ref[k:k + 1, ls], CONV_KERNEL, first_tap)
            y = acc + bias_ref[:, ls]
            mu = jnp.mean(y, axis=-1, keepdims=True)
            dlt = y - mu
            var = jnp.mean(dlt * dlt, axis=-1, keepdims=True)
            yn = dlt * lax.rsqrt(var + EPS) * lg_ref[:, ls] + lb_ref[:, ls]
            o_ref[pl.ds(r0, CHUNK), ls] = _silu(yn).astype(BF16)
        return c

    lax.fori_loop(0, n_blk, body, 0)


def _conformer_conv(p, w_dw, b_dw, ln_g, ln_b, nb, lp, width):
    r = p.shape[0]
    nch = lp // CHUNK
    t_rows = CHUNK * _divisor_tile(nch, CONV_CHUNKS, 1)
    nt = lp // t_rows
    assert width // LANES == CONV_GROUPS
    kern = functools.partial(_conv_kernel, t_rows=t_rows, width=width)
    vec = lambda b, t: (0, 0)
    return pl.pallas_call(
        kern,
        grid=(nb, nt),
        in_specs=[
            pl.BlockSpec((t_rows, width), lambda b, t: (b * nt + t, 0)),
            pl.BlockSpec((t_rows, width), lambda b, t: (b * nt + t, 1)),
            pl.BlockSpec((CONV_KERNEL, width), vec),
            pl.BlockSpec((1, width), vec),
            pl.BlockSpec((1, width), vec),
            pl.BlockSpec((1, width), vec),
        ],
        out_specs=pl.BlockSpec((t_rows, width), lambda b, t: (b * nt + t, 0)),
        out_shape=jax.ShapeDtypeStruct((r, width), BF16),
        scratch_shapes=[pltpu.VMEM((t_rows + CONV_HIST, width), F32)],
        compiler_params=_params(("arbitrary", "arbitrary")),
        name="conformer_conv",
    )(p, p, w_dw, b_dw.reshape(1, width), ln_g.reshape(1, width), ln_b.reshape(1, width))


def _delta_kernel(q_ref, k_ref, v_ref, z_ref, bdc_ref, bdr_ref, scw_ref, alr_ref, dtr_ref, alc_ref, dtc_ref,
                  ng_ref, o_ref, qbuf, kbuf, vbuf, s_sc, *, t_rows, pad, width, nh):
    t = pl.program_id(1)
    n_blk = t_rows // CHUNK
    bufs = (qbuf, kbuf, vbuf)
    srcs = (q_ref, k_ref, v_ref)

    @pl.when(t == 0)
    def _():
        s_sc[...] = jnp.zeros(s_sc.shape, F32)
        for buf in bufs:
            buf[0:SC_HIST, :] = jnp.zeros((SC_HIST, width), F32)

    @pl.when(t > 0)
    def _():
        for buf in bufs:
            buf[0:SC_HIST, :] = buf[t_rows:t_rows + SC_HIST, :]

    def copy_body(r, c):
        r0 = pl.multiple_of(r * CHUNK, CHUNK)
        for buf, src in zip(bufs, srcs):
            buf[pl.ds(SC_HIST + r0, CHUNK), :] = src[pl.ds(r0, CHUNK), :]
        return c

    lax.fori_loop(0, n_blk, copy_body, 0)

    ri = lax.broadcasted_iota(jnp.int32, (CHUNK, CHUNK), 0)
    ci = lax.broadcasted_iota(jnp.int32, (CHUNK, CHUNK), 1)
    incl = ri >= ci
    strict = ri > ci
    tril = incl.astype(BF16)
    triu = (ri <= ci).astype(BF16)
    eye = (ri == ci).astype(F32)

    def head_expand(lanes):
        eh = lax.broadcasted_iota(jnp.int32, (nh, nh * lanes), 0)
        ec = lax.broadcasted_iota(jnp.int32, (nh, nh * lanes), 1)
        return ((ec >= eh * lanes) & (ec < (eh + 1) * lanes)).astype(BF16)

    expand_c = head_expand(CHUNK)
    expand_d = head_expand(HEAD_DIM)
    first_tap = SC_HIST - (SHORT_CONV - 1)
    q_scale = HEAD_DIM ** -0.5
    heads = range(nh)

    def short_conv(buf, part, r0):
        win = buf[pl.ds(r0, CHUNK + SC_HIST), :]
        acc = _causal_taps(win, lambda k: scw_ref[k:k + 1, part * width:(part + 1) * width], SHORT_CONV, first_tap)
        return _silu(acc)

    hd = [slice(h * HEAD_DIM, (h + 1) * HEAD_DIM) for h in heads]
    hc = [slice(h * CHUNK, (h + 1) * CHUNK) for h in heads]

    def chunk_inputs(c):
        r0 = pl.multiple_of(c * CHUNK, CHUNK)
        lrow = t * t_rows + r0 + lax.broadcasted_iota(jnp.int32, (CHUNK, 1), 0)
        lcol = t * t_rows + r0 + lax.broadcasted_iota(jnp.int32, (1, CHUNK), 1)
        bl = bdc_ref[pl.ds(r0, CHUNK), :]
        br = bdr_ref[c]
        beta_col = jnp.where(lrow >= pad, jax.nn.sigmoid(bl[:, 0:nh]), 0.0)
        beta_row = jnp.where(lcol >= pad, jax.nn.sigmoid(br[0:nh, :]), 0.0)
        g_col = jnp.where(lrow >= pad, -jnp.exp(alr_ref[...]) * _softplus(bl[:, nh:2 * nh] + dtr_ref[...]), 0.0)
        g_row = jnp.where(lcol >= pad, -jnp.exp(alc_ref[...]) * _softplus(br[nh:2 * nh, :] + dtc_ref[...]), 0.0)
        gc_col = _dot_split(tril, g_col, 3, split_lhs=False)
        gc_row = _dot_split(g_row, triu, 3, split_lhs=True)
        gc_x = _dot_split(gc_col, expand_c, 3, split_lhs=True)
        qc = short_conv(qbuf, 0, r0)
        kc = short_conv(kbuf, 1, r0)
        vc = short_conv(vbuf, 2, r0)
        qn = [qc[:, s] * lax.rsqrt(jnp.sum(qc[:, s] * qc[:, s], axis=-1, keepdims=True) + EPS) * q_scale for s in hd]
        kn = [kc[:, s] * lax.rsqrt(jnp.sum(kc[:, s] * kc[:, s], axis=-1, keepdims=True) + EPS) for s in hd]
        return dict(
            r0=r0, beta_row=beta_row, eg_row=jnp.exp(gc_row), ekd_row=jnp.exp(gc_row[:, CHUNK - 1:CHUNK] - gc_row),
            beta_x=_dot_split(beta_col, expand_c, 2, split_lhs=True),
            eg_x=_dot_split(jnp.exp(gc_col), expand_d, 2, split_lhs=True),
            qn=qn, kn=kn, k16=[k.astype(BF16) for k in kn], v16=[vc[:, s].astype(BF16) for s in hd],
            decay=[jnp.where(incl, jnp.exp(jnp.where(incl, gc_x[:, hc[h]] - gc_row[h:h + 1, :], 0.0)), 0.0) for h in heads],
            zc=z_ref[pl.ds(r0, CHUNK), :])

    def process(chunks):
        cin = [chunk_inputs(c) for c in chunks]
        pairs = [(ci, h) for ci in range(len(chunks)) for h in heads]
        kq = [_dot_nt(jnp.concatenate([cin[ci]["k16"][h], cin[ci]["qn"][h].astype(BF16)], axis=0), cin[ci]["k16"][h])
              for ci, h in pairs]
        xp = [jnp.where(strict, -(kq[n][0:CHUNK] * cin[ci]["beta_x"][:, hc[h]] * cin[ci]["decay"][h]), 0.0)
              for n, (ci, h) in enumerate(pairs)]
        intra = [kq[n][CHUNK:2 * CHUNK] * cin[ci]["decay"][h] for n, (ci, h) in enumerate(pairs)]
        ainv = [eye + x for x in xp]
        n_sq = 1
        while 2 * n_sq < CHUNK:
            xp16 = [x.astype(BF16) for x in xp]
            xp = [_dot(x, x) for x in xp16]
            ainv = [a + _dot(a.astype(BF16), x.astype(BF16)) for a, x in zip(ainv, xp)]
            n_sq *= 2
        u = [_dot((ainv[n] * cin[ci]["beta_row"][h:h + 1, :]).astype(BF16), cin[ci]["v16"][h]) for n, (ci, h) in enumerate(pairs)]
        w = [_dot((ainv[n] * (cin[ci]["beta_row"][h:h + 1, :] * cin[ci]["eg_row"][h:h + 1, :])).astype(BF16), cin[ci]["k16"][h])
             for n, (ci, h) in enumerate(pairs)]
        for ci, cc in enumerate(cin):
            base = ci * nh
            q_dec = [cc["qn"][h] * cc["eg_x"][:, hd[h]] for h in heads]
            kd_t = [cc["kn"][h].T * cc["ekd_row"][h:h + 1, :] for h in heads]
            s_old = [s_sc[h] for h in heads]
            wq_s = [_dot(jnp.concatenate([w[base + h], q_dec[h]], axis=0).astype(BF16), s_old[h].astype(BF16)) for h in heads]
            v_new = [u[base + h] - wq_s[h][0:CHUNK] for h in heads]
            iv = [_dot(jnp.concatenate([intra[base + h], kd_t[h]], axis=0).astype(BF16), v_new[h].astype(BF16)) for h in heads]
            for h in heads:
                o = wq_s[h][CHUNK:2 * CHUNK] + iv[h][0:CHUNK]
                s_sc[h] = s_old[h] * cc["eg_x"][CHUNK - 1:CHUNK, hd[h]] + iv[h][CHUNK:CHUNK + HEAD_DIM]
                on = o * lax.rsqrt(jnp.mean(o * o, axis=-1, keepdims=True) + EPS) * ng_ref[...]
                o_ref[pl.ds(cc["r0"], CHUNK), hd[h]] = (on * _silu(cc["zc"][:, hd[h]])).astype(BF16)

    n_grp = n_blk // DELTA_INTERLEAVE

    def body(g, carry):
        process([g * DELTA_INTERLEAVE + k for k in range(DELTA_INTERLEAVE)])
        return carry

    lax.fori_loop(0, n_grp, body, 0)
    if n_blk % DELTA_INTERLEAVE:
        process(list(range(n_grp * DELTA_INTERLEAVE, n_blk)))


def _gated_deltanet(p, bd_col, bd_row3, sc_w, a_log, dt_bias, norm_g, nb, lp, pad, width, col0):
    r = p.shape[0]
    nh = width // HEAD_DIM
    nch = lp // CHUNK
    cpt = _divisor_tile(nch, DELTA_CHUNKS, 1)
    t_rows = CHUNK * cpt
    nt = lp // t_rows
    cb = col0 // width
    assert col0 % width == 0
    kern = functools.partial(_delta_kernel, t_rows=t_rows, pad=pad, width=width, nh=nh)
    vec = lambda b, t: (0, 0)
    part = lambda off: pl.BlockSpec((t_rows, width), lambda b, t: (b * nt + t, cb + off))
    return pl.pallas_call(
        kern,
        grid=(nb, nt),
        in_specs=[
            part(0), part(1), part(2), part(3),
            pl.BlockSpec((t_rows, 2 * nh), lambda b, t: (b * nt + t, 0)),
            pl.BlockSpec((cpt, 2 * nh, CHUNK), lambda b, t: (b * nt + t, 0, 0)),
            pl.BlockSpec((SHORT_CONV, 3 * width), vec),
            pl.BlockSpec((1, nh), vec),
            pl.BlockSpec((1, nh), vec),
            pl.BlockSpec((nh, 1), vec),
            pl.BlockSpec((nh, 1), vec),
            pl.BlockSpec((1, HEAD_DIM), vec),
        ],
        out_specs=pl.BlockSpec((t_rows, width), lambda b, t: (b * nt + t, 0)),
        out_shape=jax.ShapeDtypeStruct((r, width), BF16),
        scratch_shapes=[
            pltpu.VMEM((t_rows + SC_HIST, width), F32),
            pltpu.VMEM((t_rows + SC_HIST, width), F32),
            pltpu.VMEM((t_rows + SC_HIST, width), F32),
            pltpu.VMEM((nh, HEAD_DIM, HEAD_DIM), F32),
        ],
        compiler_params=_params(("arbitrary", "arbitrary")),
        name="gated_deltanet",
    )(p, p, p, p, bd_col, bd_row3, sc_w, a_log.reshape(1, nh), dt_bias.reshape(1, nh),
      a_log.reshape(nh, 1), dt_bias.reshape(nh, 1), norm_g.reshape(1, HEAD_DIM))


def _first_argmax(vals, iota, n):
    m = jnp.max(vals, axis=-1, keepdims=True)
    idx = jnp.min(jnp.where(vals == m, iota, n), axis=-1, keepdims=True)
    return m, idx


def _outproj_router_kernel(yc_ref, yd_ref, h_ref, wout_hbm, g_ref, w_ref, b_ref, h2_ref, u_ref, route_ref, cnt_ref,
                           carry_sc, w16, stage, sems, *, layer):
    kw = yc_ref.shape[1]
    rows = stage.shape[1]
    n_chunk = w16.shape[0] // rows

    @pl.when(pl.program_id(0) == 0)
    def _():
        carry_sc[...] = jnp.zeros(carry_sc.shape, F32)

        def chunk_copy(c):
            return pltpu.make_async_copy(wout_hbm.at[layer, pl.ds(c * rows, rows)], stage.at[c % 2], sems.at[c % 2])

        chunk_copy(0).start()
        for c in range(n_chunk):
            if c + 1 < n_chunk:
                chunk_copy(c + 1).start()
            chunk_copy(c).wait()
            w16[c * rows:(c + 1) * rows, :] = stage[c % 2].astype(BF16)

    x = h_ref[...] + (_dot(yc_ref[...], w16[0:kw, :]) + _dot(yd_ref[...], w16[kw:2 * kw, :]))
    h2_ref[...] = x
    ms = jnp.mean(x * x, axis=-1, keepdims=True)
    u = x * lax.rsqrt(ms + EPS) * g_ref[...]
    w = w_ref[...]
    nl = b_ref.shape[1]
    uh = u.astype(BF16)
    ul = (u - uh.astype(F32)).astype(BF16)
    u_ref[...] = _pack_bf16_pairs(uh)
    w_hi = w.astype(BF16).astype(F32)
    lane = lax.broadcasted_iota(jnp.int32, w.shape, 1)
    w_hl = jnp.where(lane < ROUTE_LO_LANE, w_hi, w - w_hi).astype(BF16)
    by_hi = _dot(uh, w_hl)
    by_lo = _dot(ul, w_hl)
    logits = by_hi[:, 0:nl] + (by_hi[:, ROUTE_LO_LANE:ROUTE_LO_LANE + nl] + by_lo[:, 0:nl]) + b_ref[...]
    tm = x.shape[0]
    glog = logits[:, 0:N_GROUPS]
    elog = logits[:, N_GROUPS:N_GROUPS + N_EXPERTS]
    gi = lax.broadcasted_iota(jnp.int32, (tm, N_GROUPS), 1)
    gmax, gsel = _first_argmax(glog, gi, N_GROUPS)
    p_group = 1.0 / jnp.sum(jnp.exp(glog - gmax), axis=-1, keepdims=True)
    ei = lax.broadcasted_iota(jnp.int32, (tm, N_EXPERTS), 1)
    in_group = (ei >= gsel * EXPERTS_PER_GROUP) & (ei < (gsel + 1) * EXPERTS_PER_GROUP)
    neg = jnp.float32(-jnp.inf)
    cand = jnp.where(in_group, elog, neg)
    m1, i1 = _first_argmax(cand, ei, N_EXPERTS)
    cand2 = jnp.where(ei == i1, neg, cand)
    m2, i2 = _first_argmax(cand2, ei, N_EXPERTS)
    e2 = jnp.exp(m2 - m1)
    w1 = p_group / (1.0 + e2)
    w2 = p_group * e2 / (1.0 + e2)
    oh1 = (ei == i1).astype(F32)
    oh2 = (ei == i2).astype(F32)
    oh = oh1 + oh2
    ri = lax.broadcasted_iota(jnp.int32, (tm, tm), 0)
    ci = lax.broadcasted_iota(jnp.int32, (tm, tm), 1)
    before = _dot((ri > ci).astype(BF16), oh.astype(BF16)) + carry_sc[...]
    r1 = jnp.sum(before * oh1, axis=-1, keepdims=True)
    r2 = jnp.sum(before * oh2, axis=-1, keepdims=True)
    carry_sc[...] += jnp.sum(oh, axis=0, keepdims=True)
    cnt_ref[...] = carry_sc[...]
    li = lax.broadcasted_iota(jnp.int32, (tm, ROUTE_COLS), 1)
    rec = jnp.zeros((tm, ROUTE_COLS), F32)
    for k, col in enumerate((i1.astype(F32), i2.astype(F32), r1, r2, w1, w2)):
        rec = jnp.where(li == k, col, rec)
    route_ref[...] = rec


def _outproj_router(y_conv, y_delta, w_out, layer, h, norm_g, w_group, b_group, w_router, b_router):
    r, d = h.shape
    kw = y_conv.shape[1]
    tm = _divisor_tile(r, ROUTER_TM, 16)
    w = jnp.concatenate([w_group, w_router], axis=1)
    b = jnp.concatenate([b_group, b_router]).reshape(1, -1)
    nl = w.shape[1]
    assert nl <= ROUTE_LO_LANE
    gap = jnp.zeros((d, ROUTE_LO_LANE - nl), w.dtype)
    w = jnp.concatenate([w, gap, w, gap], axis=1)
    stage_rows = _divisor_tile(2 * kw, WOUT_STAGE_ROWS, SUBLANES)
    kern = functools.partial(_outproj_router_kernel, layer=layer)
    rowblk = lambda width: pl.BlockSpec((tm, width), lambda i: (i, 0))
    const = lambda shape: pl.BlockSpec(shape, lambda i: (0, 0))
    return pl.pallas_call(
        kern,
        grid=(r // tm,),
        in_specs=[rowblk(kw), rowblk(kw), rowblk(d), pl.BlockSpec(memory_space=pl.ANY),
                  const((1, d)), const((d, 2 * ROUTE_LO_LANE)), const((1, nl))],
        out_specs=[rowblk(d), rowblk(d // 2), rowblk(ROUTE_COLS), const((1, N_EXPERTS))],
        out_shape=[
            jax.ShapeDtypeStruct((r, d), F32),
            jax.ShapeDtypeStruct((r, d // 2), jnp.uint32),
            jax.ShapeDtypeStruct((r, ROUTE_COLS), F32),
            jax.ShapeDtypeStruct((1, N_EXPERTS), F32),
        ],
        scratch_shapes=[
            pltpu.VMEM((1, N_EXPERTS), F32),
            pltpu.VMEM((2 * kw, d), BF16),
            pltpu.VMEM((2, stage_rows, d), F32),
            pltpu.SemaphoreType.DMA((2,)),
        ],
        compiler_params=_params(("arbitrary",)),
        name="outproj_router",
    )(y_conv, y_delta, h, w_out, norm_g.reshape(1, d), w, b)


def _dispatch_plan(route, cnt, tm, n_tiles):
    cnt = cnt[0].astype(jnp.int32)
    padded = ((cnt + tm - 1) // tm) * tm
    ends = jnp.cumsum(padded)
    off = ends - padded
    onehot = (route[:, 0:TOP_K, None] == jnp.arange(N_EXPERTS, dtype=F32)).astype(F32)
    pos = jnp.einsum("rke,e->rk", onehot, off.astype(F32), precision=HI) + route[:, TOP_K:2 * TOP_K]
    pos = pos.astype(jnp.int32)
    pos0, pos1 = pos[:, 0], pos[:, 1]
    n_used = ends[-1] // tm
    tiles = jnp.arange(n_tiles, dtype=jnp.int32)
    tile_e = jnp.sum((tiles[:, None] * tm >= ends[None, :]).astype(jnp.int32), axis=1)
    tile_e = jnp.minimum(jnp.where(tiles < n_used, tile_e, tile_e[n_used - 1]), N_EXPERTS - 1)
    experts = jnp.arange(N_EXPERTS, dtype=jnp.int32)
    has_rows = cnt > 0
    slot_tab = (jnp.cumsum(has_rows.astype(jnp.int32)) - 1) % 2
    later = jnp.where(has_rows[None, :] & (experts[None, :] > experts[:, None]), experts[None, :], N_EXPERTS)
    next_tab = jnp.min(later, axis=1)
    next_tab = jnp.where(next_tab == N_EXPERTS, -1, next_tab)
    sel = (tile_e[:, None] == experts[None, :]).astype(jnp.int32)
    plan = dict(tile_e=tile_e, slot=jnp.sum(sel * slot_tab[None, :], axis=1), next_e=jnp.sum(sel * next_tab[None, :], axis=1),
                n_used=n_used.reshape(1), pad_start=off + cnt, pad_end=ends)
    return pos0, pos1, plan


def _dispatch_kernel(pos0_ref, pos1_ref, ps_ref, pe_ref, nu_ref, u_ref, xs_ref, zbuf, sem, zsem, *, t_rows, tm, n_tiles):
    i = pl.program_id(0)

    @pl.when(i == 0)
    def _():
        zbuf[...] = jnp.zeros(zbuf.shape, zbuf.dtype)

        def zero_copy(start, n):
            return pltpu.make_async_copy(zbuf.at[pl.ds(0, n)], xs_ref.at[pl.ds(start, n)], zsem)

        def pad_pieces(e):
            start, end = ps_ref[e], pe_ref[e]
            aligned = jnp.minimum((start + (SUBLANES - 1)) & (-SUBLANES), end)
            pieces = [(start + k < aligned, start + k, 1) for k in range(SUBLANES - 1)]
            rem = end - aligned
            at = aligned
            n = tm // 2
            while n >= SUBLANES:
                pieces.append(((rem & n) != 0, pl.multiple_of(at, SUBLANES), n))
                at = at + (rem & n)
                n //= 2
            return pieces

        def for_each_piece(action):
            def pad_body(e, c):
                for take, start, n in pad_pieces(e):
                    pl.when(take)(functools.partial(action, start, n))
                return c

            def tail_body(j, c):
                pl.when(j >= nu_ref[0])(functools.partial(action, pl.multiple_of(j * tm, SUBLANES), tm))
                return c

            lax.fori_loop(0, N_EXPERTS, pad_body, 0)
            lax.fori_loop(0, n_tiles, tail_body, 0)

        for_each_piece(lambda start, n: zero_copy(start, n).start())
        for_each_piece(lambda start, n: zero_copy(start, n).wait())

    base = i * t_rows

    def body(r, c):
        src = u_ref.at[pl.ds(r, 1)]
        pltpu.make_async_copy(src, xs_ref.at[pl.ds(pos0_ref[base + r], 1)], sem).start()
        pltpu.make_async_copy(src, xs_ref.at[pl.ds(pos1_ref[base + r], 1)], sem).start()
        return c

    lax.fori_loop(0, t_rows, body, 0, unroll=DMA_UNROLL)
    for _ in range(2):
        pltpu.make_async_copy(u_ref, xs_ref.at[pl.ds(0, t_rows)], sem).wait()


def _dispatch(u, pos0, pos1, plan, n_slots, tm):
    r, d = u.shape
    t_rows = _divisor_tile(r, DISPATCH_T, SUBLANES)
    n_tiles = n_slots // tm
    assert tm & (tm - 1) == 0 and tm >= 2 * SUBLANES
    kern = functools.partial(_dispatch_kernel, t_rows=t_rows, tm=tm, n_tiles=n_tiles)
    return pl.pallas_call(
        kern,
        grid_spec=pltpu.PrefetchScalarGridSpec(
            num_scalar_prefetch=5,
            grid=(r // t_rows,),
            in_specs=[pl.BlockSpec((t_rows, d), lambda i, *_: (i, 0))],
            out_specs=pl.BlockSpec(memory_space=pl.ANY),
            scratch_shapes=[pltpu.VMEM((tm, d), u.dtype), pltpu.SemaphoreType.DMA(()), pltpu.SemaphoreType.DMA(())],
        ),
        out_shape=jax.ShapeDtypeStruct((n_slots, d), u.dtype),
        compiler_params=_params(("arbitrary",)),
        name="moe_dispatch",
    )(pos0, pos1, plan["pad_start"], plan["pad_end"], plan["n_used"], u)


def _expert_kernel(te_ref, slot_ref, nxt_ref, nu_ref, x_ref, wg_hbm, wu_hbm, wd_hbm, y_ref,
                   wg_f, wu_f, wd_f, wg_b, wu_b, wd_b, sems, *, layer):
    i = pl.program_id(0)
    e = te_ref[i]
    slot = slot_ref[i]
    used = i < nu_ref[0]
    first_of_expert = used & ((i == 0) | (te_ref[jnp.maximum(i - 1, 0)] != e))

    def weight_copies(expert, s):
        return [pltpu.make_async_copy(src.at[layer, expert], dst.at[s], sems.at[s])
                for src, dst in ((wg_hbm, wg_f), (wu_hbm, wu_f), (wd_hbm, wd_f))]

    @pl.when(i == 0)
    def _():
        for c in weight_copies(e, slot):
            c.start()

    @pl.when(first_of_expert)
    def _():
        for c in weight_copies(e, slot):
            c.wait()
        nxt = nxt_ref[i]

        @pl.when(nxt >= 0)
        def _():
            for c in weight_copies(nxt, 1 - slot):
                c.start()

        wg_b[...] = wg_f[slot].astype(BF16)
        wu_b[...] = wu_f[slot].astype(BF16)
        wd_b[...] = wd_f[slot].astype(BF16)

    @pl.when(used)
    def _():
        xa, xb = _unpack_bf16_pairs(x_ref[...])
        half = xa.shape[1]
        hg = _dot(xa, wg_b[0:half, :]) + _dot(xb, wg_b[half:2 * half, :])
        hu = _dot(xa, wu_b[0:half, :]) + _dot(xb, wu_b[half:2 * half, :])
        hid = (_silu(hg) * hu).astype(BF16)
        y_ref[...] = _pack_bf16_pairs(_dot(hid, wd_b[...]).astype(BF16))

    @pl.when(jnp.logical_not(used))
    def _():
        y_ref[...] = jnp.zeros(y_ref.shape, y_ref.dtype)


def _experts(xs, plan, w_gate, w_up, w_down, layer, tm):
    n_slots = xs.shape[0]
    _, _, d, f = w_gate.shape
    assert xs.shape[1] * 2 == d
    n_tiles = n_slots // tm
    used_row = lambda i, te, sl, nx, nu: (jnp.minimum(i, nu[0] - 1), 0)
    hbm = pl.BlockSpec(memory_space=pl.ANY)
    kern = functools.partial(_expert_kernel, layer=layer)
    return pl.pallas_call(
        kern,
        grid_spec=pltpu.PrefetchScalarGridSpec(
            num_scalar_prefetch=4,
            grid=(n_tiles,),
            in_specs=[pl.BlockSpec((tm, d // 2), used_row), hbm, hbm, hbm],
            out_specs=pl.BlockSpec((tm, d // 2), lambda i, te, sl, nx, nu: (i, 0)),
            scratch_shapes=[
                pltpu.VMEM((2, d, f), F32), pltpu.VMEM((2, d, f), F32), pltpu.VMEM((2, f, d), F32),
                pltpu.VMEM((d, f), BF16), pltpu.VMEM((d, f), BF16), pltpu.VMEM((f, d), BF16),
                pltpu.SemaphoreType.DMA((2,)),
            ],
        ),
        out_shape=jax.ShapeDtypeStruct((n_slots, d // 2), jnp.uint32),
        compiler_params=_params(("arbitrary",)),
        name="moe_experts",
    )(plan["tile_e"], plan["slot"], plan["next_e"], plan["n_used"], xs, w_gate, w_up, w_down)


def _combine_kernel(pos0_ref, pos1_ref, h_ref, route_ref, ys_ref, o_ref, buf0, buf1, sems, *, t_rows, n_steps):
    i = pl.program_id(0)

    def gather_tile(step, slot):
        base = step * t_rows

        def body(r, c):
            pltpu.make_async_copy(ys_ref.at[pl.ds(pos0_ref[base + r], 1)], buf0.at[slot, pl.ds(r, 1)], sems.at[slot]).start()
            pltpu.make_async_copy(ys_ref.at[pl.ds(pos1_ref[base + r], 1)], buf1.at[slot, pl.ds(r, 1)], sems.at[slot]).start()
            return c

        lax.fori_loop(0, t_rows, body, 0, unroll=DMA_UNROLL)

    @pl.when(i == 0)
    def _():
        gather_tile(0, 0)

    @pl.when(i + 1 < n_steps)
    def _():
        gather_tile(i + 1, (i + 1) % 2)

    slot = i % 2
    for buf in (buf0, buf1):
        pltpu.make_async_copy(ys_ref.at[pl.ds(0, t_rows)], buf.at[slot], sems.at[slot]).wait()
    g = route_ref[...]
    lo0, hi0 = _unpack_bf16_pairs(buf0[slot], F32)
    lo1, hi1 = _unpack_bf16_pairs(buf1[slot], F32)
    half = lo0.shape[1]
    o_ref[:, 0:half] = h_ref[:, 0:half] + g[:, 4:5] * lo0 + g[:, 5:6] * lo1
    o_ref[:, half:2 * half] = h_ref[:, half:2 * half] + g[:, 4:5] * hi0 + g[:, 5:6] * hi1


def _combine(h, route, ys, pos0, pos1):
    r, d = h.shape
    t_rows = _divisor_tile(r, COMBINE_T, SUBLANES)
    n_steps = r // t_rows
    kern = functools.partial(_combine_kernel, t_rows=t_rows, n_steps=n_steps)
    return pl.pallas_call(
        kern,
        grid_spec=pltpu.PrefetchScalarGridSpec(
            num_scalar_prefetch=2,
            grid=(n_steps,),
            in_specs=[
                pl.BlockSpec((t_rows, d), lambda i, p0, p1: (i, 0)),
                pl.BlockSpec((t_rows, ROUTE_COLS), lambda i, p0, p1: (i, 0)),
                pl.BlockSpec(memory_space=pl.ANY),
            ],
            out_specs=pl.BlockSpec((t_rows, d), lambda i, p0, p1: (i, 0)),
            scratch_shapes=[pltpu.VMEM((2, t_rows, d // 2), ys.dtype), pltpu.VMEM((2, t_rows, d // 2), ys.dtype),
                            pltpu.SemaphoreType.DMA((2,))],
        ),
        out_shape=jax.ShapeDtypeStruct((r, d), F32),
        compiler_params=_params(("arbitrary",)),
        name="moe_combine",
    )(pos0, pos1, h, route, ys)


def _routed_experts(h, u, route, cnt, w_gate, w_up, w_down, layer):
    r = h.shape[0]
    tm = EXPERT_TM
    n_tiles = -(-(TOP_K * r + N_EXPERTS * (tm - 1)) // tm)
    pos0, pos1, plan = _dispatch_plan(route, cnt, tm, n_tiles)
    xs = _dispatch(u, pos0, pos1, plan, n_tiles * tm, tm)
    ys = _experts(xs, plan, w_gate, w_up, w_down, layer, tm)
    return _combine(h, route, ys, pos0, pos1)


def _final_kernel(h_ref, g_ref, o_ref):
    x = h_ref[...]
    ms = jnp.mean(x * x, axis=-1, keepdims=True)
    o_ref[0] = x * lax.rsqrt(ms + EPS) * g_ref[...]


def _final_norm(h, norm_g, nb, lp, seq, skip):
    d = h.shape[1]
    t_rows = _divisor_tile(seq, FINAL_T, SUBLANES)
    assert skip % SUBLANES == 0 and lp % SUBLANES == 0
    first_row = lambda b, t: (pl.multiple_of(b * lp + skip + t * t_rows, SUBLANES), 0)
    return pl.pallas_call(
        _final_kernel,
        grid=(nb, seq // t_rows),
        in_specs=[
            pl.BlockSpec((pl.Element(t_rows), pl.Element(d)), first_row),
            pl.BlockSpec((1, d), lambda b, t: (0, 0)),
        ],
        out_specs=pl.BlockSpec((1, t_rows, d), lambda b, t: (b, t, 0)),
        out_shape=jax.ShapeDtypeStruct((nb, seq, d), F32),
        compiler_params=_params(("arbitrary", "arbitrary")),
        name="final_norm",
    )(h, norm_g.reshape(1, d))


def kernel(x, meta, attn_norm, w_in, conv_dw_w, conv_dw_b, conv_ln_g, conv_ln_b, short_conv_w, a_log, dt_bias,
           delta_norm_g, w_out, ffn_norm, w_group, b_group, w_router, b_router, w_gate, w_up, w_down, final_norm):
    nb, seq, d = x.shape
    depth = w_in.shape[0]
    conv_w = conv_dw_w.shape[2]
    delta_w = short_conv_w.shape[2] // 3
    nh = delta_w // HEAD_DIM
    n_main = 2 * conv_w + 4 * delta_w
    assert w_in.shape[2] == n_main + 2 * nh and conv_w == delta_w
    ln = N_META + seq
    pad = (-ln) % CHUNK
    lp = ln + pad
    skip = pad + N_META

    meta_b = jnp.broadcast_to(meta[None].astype(x.dtype), (nb, N_META, d))
    h = jnp.concatenate([jnp.zeros((nb, pad, d), x.dtype), meta_b, x], axis=1).reshape(nb * lp, d)

    w_in_t = jnp.swapaxes(w_in, 1, 2)
    for l in range(depth):
        p, bd_col, bd_row = _inproj(h, attn_norm[l], w_in_t, l, n_main, lp, pad, nb)
        bd_row3 = bd_row.reshape(2 * nh, nb * lp // CHUNK, CHUNK).transpose(1, 0, 2)
        y_conv = _conformer_conv(p, conv_dw_w[l], conv_dw_b[l], conv_ln_g[l], conv_ln_b[l], nb, lp, conv_w)
        y_delta = _gated_deltanet(p, bd_col, bd_row3, short_conv_w[l], a_log[l], dt_bias[l], delta_norm_g[l],
                                  nb, lp, pad, delta_w, 2 * conv_w)
        h, u, route, cnt = _outproj_router(y_conv, y_delta, w_out, l, h, ffn_norm[l], w_group[l], b_group[l],
                                           w_router[l], b_router[l])
        h = _routed_experts(h, u, route, cnt, w_gate, w_up, w_down, l)
    return _final_norm(h, final_norm, nb, lp, seq, skip)
```

```python
import functools

import jax
import jax.numpy as jnp
from jax import lax
from jax.experimental import pallas as pl
from jax.experimental.pallas import tpu as pltpu

F32 = jnp.float32
BF16 = jnp.bfloat16
HI = lax.Precision.HIGHEST

EPS = 1e-6
MASKED_EXPONENT = -1e30
CHUNK = 64
N_META = 16
CONV_GROUPS = 8
CONV_KERNEL = 31
HEAD_DIM = 128
SHORT_CONV = 4
N_GROUPS = 4
EXPERTS_PER_GROUP = 8
N_EXPERTS = N_GROUPS * EXPERTS_PER_GROUP
TOP_K = 2
ROUTE_COLS = 8
ROUTE_LO_LANE = 64
LANES = 128
SUBLANES = 8
CONV_HIST = 32
SC_HIST = 8
VMEM_LIMIT = 56 * 1024 * 1024

INPROJ_TM, INPROJ_TN = 1664, 512
CONV_CHUNKS = 13
DELTA_CHUNKS = 5
DELTA_INTERLEAVE = 3
WOUT_STAGE_ROWS = 512
ROUTER_TM = 640
EXPERT_TM = 256
DISPATCH_T = 640
COMBINE_T = 640
DMA_UNROLL = 8
FINAL_T = 512


def _divisor_tile(n, cap, mult):
    best = None
    for t in range(mult, min(n, cap) + 1, mult):
        if n % t == 0:
            best = t
    if best is None:
        raise ValueError(f"no tile for n={n} cap={cap} mult={mult}")
    return best


def _params(sem):
    return pltpu.CompilerParams(dimension_semantics=sem, vmem_limit_bytes=VMEM_LIMIT)


def _dot(a, b, precision=None):
    return jnp.dot(a, b, preferred_element_type=F32, precision=precision)


def _dot_nt(a, b, precision=None):
    return lax.dot_general(a, b, (((1,), (1,)), ((), ())), preferred_element_type=F32, precision=precision)


def _silu(x):
    return x * jax.nn.sigmoid(x)


def _softplus(x):
    return jnp.maximum(x, 0.0) + jnp.log1p(jnp.exp(-jnp.abs(x)))


def _dot_split(a, b, n_parts, split_lhs):
    x = a if split_lhs else b
    acc = None
    for _ in range(n_parts):
        piece = x.astype(BF16)
        term = _dot(piece, b) if split_lhs else _dot(a, piece)
        acc = term if acc is None else acc + term
        x = x - piece.astype(F32)
    return acc


def _pack_bf16_pairs(x):
    n = x.shape[1] // 2
    lo = lax.bitcast_convert_type(x[:, 0:n].astype(F32), jnp.uint32)
    hi = lax.bitcast_convert_type(x[:, n:2 * n].astype(F32), jnp.uint32)
    return (lo >> 16) | hi


def _unpack_bf16_pairs(w, dtype=BF16):
    lo = lax.bitcast_convert_type(w << 16, F32).astype(dtype)
    hi = lax.bitcast_convert_type(w & jnp.uint32(0xFFFF0000), F32).astype(dtype)
    return lo, hi


def _causal_taps(win, tap_w, n_taps, first_tap):
    rows = win.shape[0]
    acc = jnp.zeros((CHUNK,) + win.shape[1:], F32)
    for res in range(SUBLANES):
        offs = [o for o in range(first_tap, first_tap + n_taps) if o % SUBLANES == res]
        if not offs:
            continue
        shifted = pltpu.roll(win, rows - res, axis=0) if res else win
        for o in offs:
            assert o + CHUNK <= rows
            a = o - res
            acc = acc + tap_w(o - first_tap) * shifted[a:a + CHUNK, :]
    return acc


def _inproj_kernel(h_ref, g_ref, w_ref, wbd_ref, p_ref, bdc_ref, bdr_ref, u_sc, *, tm, rb, lp, pad, nb):
    i = pl.program_id(0)
    j = pl.program_id(1)

    @pl.when(j == 0)
    def _():
        wbd = wbd_ref[...].astype(BF16)
        for blk in range(tm // rb):
            rows = slice(blk * rb, (blk + 1) * rb)
            x = h_ref[rows, :]
            ms = jnp.mean(x * x, axis=-1, keepdims=True)
            u = x * lax.rsqrt(ms + EPS) * g_ref[...]
            row = i * tm + blk * rb + lax.broadcasted_iota(jnp.int32, (rb, 1), 0)
            valid = (row >= pad) & (row < lp)
            for b in range(1, nb):
                valid = valid | ((row >= b * lp + pad) & (row < (b + 1) * lp))
            ub = jnp.where(valid, u, 0.0).astype(BF16)
            u_sc[rows, :] = ub
            bdc_ref[rows, :] = _dot_nt(ub, wbd)
            bdr_ref[:, rows] = _dot_nt(wbd, ub)

    p_ref[...] = _dot_nt(u_sc[...], w_ref[...].astype(BF16))


def _inproj(h, norm_g, w_in_t, layer, n_main, lp, pad, nb):
    r, d = h.shape
    n_bd = w_in_t.shape[1] - n_main
    tm = _divisor_tile(r, INPROJ_TM, LANES)
    tn = _divisor_tile(n_main, INPROJ_TN, LANES)
    assert n_main % n_bd == 0 and n_bd % SUBLANES == 0
    kern = functools.partial(_inproj_kernel, tm=tm, rb=LANES, lp=lp, pad=pad, nb=nb)
    return pl.pallas_call(
        kern,
        grid=(r // tm, n_main // tn),
        in_specs=[
            pl.BlockSpec((tm, d), lambda i, j: (i, 0)),
            pl.BlockSpec((1, d), lambda i, j: (0, 0)),
            pl.BlockSpec((None, tn, d), lambda i, j: (layer, j, 0)),
            pl.BlockSpec((None, n_bd, d), lambda i, j: (layer, n_main // n_bd, 0)),
        ],
        out_specs=[
            pl.BlockSpec((tm, tn), lambda i, j: (i, j)),
            pl.BlockSpec((tm, n_bd), lambda i, j: (i, 0)),
            pl.BlockSpec((n_bd, tm), lambda i, j: (0, i)),
        ],
        out_shape=[
            jax.ShapeDtypeStruct((r, n_main), F32),
            jax.ShapeDtypeStruct((r, n_bd), F32),
            jax.ShapeDtypeStruct((n_bd, r), F32),
        ],
        scratch_shapes=[pltpu.VMEM((tm, d), BF16)],
        compiler_params=_params(("arbitrary", "arbitrary")),
        name="inproj",
    )(h, norm_g.reshape(1, d), w_in_t, w_in_t)


def _conv_kernel(a_ref, b_ref, w_ref, bias_ref, lg_ref, lb_ref, o_ref, ybuf, *, t_rows, width):
    t = pl.program_id(1)
    n_blk = t_rows // CHUNK

    @pl.when(t == 0)
    def _():
        ybuf[0:CONV_HIST, :] = jnp.zeros((CONV_HIST, width), F32)

    @pl.when(t > 0)
    def _():
        ybuf[0:CONV_HIST, :] = ybuf[t_rows:t_rows + CONV_HIST, :]

    def glu_body(r, c):
        r0 = pl.multiple_of(r * CHUNK, CHUNK)
        a = a_ref[pl.ds(r0, CHUNK), :]
        g = b_ref[pl.ds(r0, CHUNK), :]
        ybuf[pl.ds(CONV_HIST + r0, CHUNK), :] = a * jax.nn.sigmoid(g)
        return c

    lax.fori_loop(0, n_blk, glu_body, 0)

    first_tap = CONV_HIST - (CONV_KERNEL - 1)

    def body(r, c):
        r0 = pl.multiple_of(r * CHUNK, CHUNK)
        for gi in range(width // LANES):
            ls = slice(gi * LANES, (gi + 1) * LANES)
            win = ybuf[pl.ds(r0, CHUNK + CONV_HIST), ls]
            acc = _causal_taps(win, lambda k: w_ref[k:k + 1, ls], CONV_KERNEL, first_tap)
            y = acc + bias_ref[:, ls]
            mu = jnp.mean(y, axis=-1, keepdims=True)
            dlt = y - mu
            var = jnp.mean(dlt * dlt, axis=-1, keepdims=True)
            yn = dlt * lax.rsqrt(var + EPS) * lg_ref[:, ls] + lb_ref[:, ls]
            o_ref[pl.ds(r0, CHUNK), ls] = _silu(yn).astype(BF16)
        return c

    lax.fori_loop(0, n_blk, body, 0)


def _conformer_conv(p, w_dw, b_dw, ln_g, ln_b, nb, lp, width):
    r = p.shape[0]
    nch = lp // CHUNK
    t_rows = CHUNK * _divisor_tile(nch, CONV_CHUNKS, 1)
    nt = lp // t_rows
    assert width // LANES == CONV_GROUPS
    kern = functools.partial(_conv_kernel, t_rows=t_rows, width=width)
    vec = lambda b, t: (0, 0)
    return pl.pallas_call(
        kern,
        grid=(nb, nt),
        in_specs=[
            pl.BlockSpec((t_rows, width), lambda b, t: (b * nt + t, 0)),
            pl.BlockSpec((t_rows, width), lambda b, t: (b * nt + t, 1)),
            pl.BlockSpec((CONV_KERNEL, width), vec),
            pl.BlockSpec((1, width), vec),
            pl.BlockSpec((1, width), vec),
            pl.BlockSpec((1, width), vec),
        ],
        out_specs=pl.BlockSpec((t_rows, width), lambda b, t: (b * nt + t, 0)),
        out_shape=jax.ShapeDtypeStruct((r, width), BF16),
        scratch_shapes=[pltpu.VMEM((t_rows + CONV_HIST, width), F32)],
        compiler_params=_params(("arbitrary", "arbitrary")),
        name="conformer_conv",
    )(p, p, w_dw, b_dw.reshape(1, width), ln_g.reshape(1, width), ln_b.reshape(1, width))


def _delta_kernel(q_ref, k_ref, v_ref, z_ref, bdc_ref, bdr_ref, scw_ref, alr_ref, dtr_ref, alc_ref, dtc_ref,
                  ng_ref, o_ref, qbuf, kbuf, vbuf, s_sc, *, t_rows, pad, width, nh):
    t = pl.program_id(1)
    n_blk = t_rows // CHUNK
    bufs = (qbuf, kbuf, vbuf)
    srcs = (q_ref, k_ref, v_ref)

    @pl.when(t == 0)
    def _():
        s_sc[...] = jnp.zeros(s_sc.shape, F32)
        for buf in bufs:
            buf[0:SC_HIST, :] = jnp.zeros((SC_HIST, width), F32)

    @pl.when(t > 0)
    def _():
        for buf in bufs:
            buf[0:SC_HIST, :] = buf[t_rows:t_rows + SC_HIST, :]

    def copy_body(r, c):
        r0 = pl.multiple_of(r * CHUNK, CHUNK)
        for buf, src in zip(bufs, srcs):
            buf[pl.ds(SC_HIST + r0, CHUNK), :] = src[pl.ds(r0, CHUNK), :]
        return c

    lax.fori_loop(0, n_blk, copy_body, 0)

    ri = lax.broadcasted_iota(jnp.int32, (CHUNK, CHUNK), 0)
    ci = lax.broadcasted_iota(jnp.int32, (CHUNK, CHUNK), 1)
    incl = ri >= ci
    strict = ri > ci
    tril = incl.astype(BF16)
    triu = (ri <= ci).astype(BF16)
    eye = (ri == ci).astype(F32)

    def head_expand(lanes):
        eh = lax.broadcasted_iota(jnp.int32, (nh, nh * lanes), 0)
        ec = lax.broadcasted_iota(jnp.int32, (nh, nh * lanes), 1)
        return ((ec >= eh * lanes) & (ec < (eh + 1) * lanes)).astype(BF16)

    expand_c = head_expand(CHUNK)
    expand_d = head_expand(HEAD_DIM)
    first_tap = SC_HIST - (SHORT_CONV - 1)
    q_scale = HEAD_DIM ** -0.5
    heads = range(nh)

    def short_conv(buf, part, r0):
        win = buf[pl.ds(r0, CHUNK + SC_HIST), :]
        acc = _causal_taps(win, lambda k: scw_ref[k:k + 1, part * width:(part + 1) * width], SHORT_CONV, first_tap)
        return _silu(acc)

    hd = [slice(h * HEAD_DIM, (h + 1) * HEAD_DIM) for h in heads]
    hc = [slice(h * CHUNK, (h + 1) * CHUNK) for h in heads]

    def chunk_inputs(c):
        r0 = pl.multiple_of(c * CHUNK, CHUNK)
        lrow = t * t_rows + r0 + lax.broadcasted_iota(jnp.int32, (CHUNK, 1), 0)
        lcol = t * t_rows + r0 + lax.broadcasted_iota(jnp.int32, (1, CHUNK), 1)
        bl = bdc_ref[pl.ds(r0, CHUNK), :]
        br = bdr_ref[c]
        beta_col = jnp.where(lrow >= pad, jax.nn.sigmoid(bl[:, 0:nh]), 0.0)
        beta_row = jnp.where(lcol >= pad, jax.nn.sigmoid(br[0:nh, :]), 0.0)
        g_col = jnp.where(lrow >= pad, -jnp.exp(alr_ref[...]) * _softplus(bl[:, nh:2 * nh] + dtr_ref[...]), 0.0)
        g_row = jnp.where(lcol >= pad, -jnp.exp(alc_ref[...]) * _softplus(br[nh:2 * nh, :] + dtc_ref[...]), 0.0)
        gc_col = _dot_split(tril, g_col, 3, split_lhs=False)
        gc_row = _dot_split(g_row, triu, 3, split_lhs=True)
        gc_x = _dot_split(gc_col, expand_c, 3, split_lhs=True)
        qc = short_conv(qbuf, 0, r0)
        kc = short_conv(kbuf, 1, r0)
        vc = short_conv(vbuf, 2, r0)
        qn = [qc[:, s] * lax.rsqrt(jnp.sum(qc[:, s] * qc[:, s], axis=-1, keepdims=True) + EPS) * q_scale for s in hd]
        kn = [kc[:, s] * lax.rsqrt(jnp.sum(kc[:, s] * kc[:, s], axis=-1, keepdims=True) + EPS) for s in hd]
        return dict(
            r0=r0, beta_row=beta_row, eg_row=jnp.exp(gc_row), ekd_row=jnp.exp(gc_row[:, CHUNK - 1:CHUNK] - gc_row),
            beta_x=_dot_split(beta_col, expand_c, 2, split_lhs=True),
            eg_x=_dot_split(jnp.exp(gc_col), expand_d, 2, split_lhs=True),
            qn=qn, kn=kn, k16=[k.astype(BF16) for k in kn], v16=[vc[:, s].astype(BF16) for s in hd],
            decay=[jnp.exp(jnp.where(incl, gc_x[:, hc[h]] - gc_row[h:h + 1, :], MASKED_EXPONENT)) for h in heads],
            zc=z_ref[pl.ds(r0, CHUNK), :])

    def process(chunks):
        cin = [chunk_inputs(c) for c in chunks]
        pairs = [(ci, h) for ci in range(len(chunks)) for h in heads]
        kq = [_dot_nt(jnp.concatenate([cin[ci]["k16"][h], cin[ci]["qn"][h].astype(BF16)], axis=0), cin[ci]["k16"][h])
              for ci, h in pairs]
        xp = [jnp.where(strict, -(kq[n][0:CHUNK] * cin[ci]["beta_x"][:, hc[h]] * cin[ci]["decay"][h]), 0.0)
              for n, (ci, h) in enumerate(pairs)]
        intra = [kq[n][CHUNK:2 * CHUNK] * cin[ci]["decay"][h] for n, (ci, h) in enumerate(pairs)]
        ainv = [eye + x for x in xp]
        n_sq = 1
        while 2 * n_sq < CHUNK:
            xp16 = [x.astype(BF16) for x in xp]
            xp = [_dot(x, x) for x in xp16]
            ainv = [a + _dot(a.astype(BF16), x.astype(BF16)) for a, x in zip(ainv, xp)]
            n_sq *= 2
        u = [_dot((ainv[n] * cin[ci]["beta_row"][h:h + 1, :]).astype(BF16), cin[ci]["v16"][h]) for n, (ci, h) in enumerate(pairs)]
        w = [_dot((ainv[n] * (cin[ci]["beta_row"][h:h + 1, :] * cin[ci]["eg_row"][h:h + 1, :])).astype(BF16), cin[ci]["k16"][h])
             for n, (ci, h) in enumerate(pairs)]
        for ci, cc in enumerate(cin):
            base = ci * nh
            q_dec = [cc["qn"][h] * cc["eg_x"][:, hd[h]] for h in heads]
            kd_t = [cc["kn"][h].T * cc["ekd_row"][h:h + 1, :] for h in heads]
            s_old = [s_sc[h] for h in heads]
            wq_s = [_dot(jnp.concatenate([w[base + h], q_dec[h]], axis=0).astype(BF16), s_old[h].astype(BF16)) for h in heads]
            v_new = [u[base + h] - wq_s[h][0:CHUNK] for h in heads]
            iv = [_dot(jnp.concatenate([intra[base + h], kd_t[h]], axis=0).astype(BF16), v_new[h].astype(BF16)) for h in heads]
            for h in heads:
                o = wq_s[h][CHUNK:2 * CHUNK] + iv[h][0:CHUNK]
                s_sc[h] = s_old[h] * cc["eg_x"][CHUNK - 1:CHUNK, hd[h]] + iv[h][CHUNK:CHUNK + HEAD_DIM]
                on = o * lax.rsqrt(jnp.mean(o * o, axis=-1, keepdims=True) + EPS) * ng_ref[...]
                o_ref[pl.ds(cc["r0"], CHUNK), hd[h]] = (on * _silu(cc["zc"][:, hd[h]])).astype(BF16)

    n_grp = n_blk // DELTA_INTERLEAVE

    def body(g, carry):
        process([g * DELTA_INTERLEAVE + k for k in range(DELTA_INTERLEAVE)])
        return carry

    lax.fori_loop(0, n_grp, body, 0)
    if n_blk % DELTA_INTERLEAVE:
        process(list(range(n_grp * DELTA_INTERLEAVE, n_blk)))


def _gated_deltanet(p, bd_col, bd_row3, sc_w, a_log, dt_bias, norm_g, nb, lp, pad, width, col0):
    r = p.shape[0]
    nh = width // HEAD_DIM
    nch = lp // CHUNK
    cpt = _divisor_tile(nch, DELTA_CHUNKS, 1)
    t_rows = CHUNK * cpt
    nt = lp // t_rows
    cb = col0 // width
    assert col0 % width == 0
    kern = functools.partial(_delta_kernel, t_rows=t_rows, pad=pad, width=width, nh=nh)
    vec = lambda b, t: (0, 0)
    part = lambda off: pl.BlockSpec((t_rows, width), lambda b, t: (b * nt + t, cb + off))
    return pl.pallas_call(
        kern,
        grid=(nb, nt),
        in_specs=[
            part(0), part(1), part(2), part(3),
            pl.BlockSpec((t_rows, 2 * nh), lambda b, t: (b * nt + t, 0)),
            pl.BlockSpec((cpt, 2 * nh, CHUNK), lambda b, t: (b * nt + t, 0, 0)),
            pl.BlockSpec((SHORT_CONV, 3 * width), vec),
            pl.BlockSpec((1, nh), vec),
            pl.BlockSpec((1, nh), vec),
            pl.BlockSpec((nh, 1), vec),
            pl.BlockSpec((nh, 1), vec),
            pl.BlockSpec((1, HEAD_DIM), vec),
        ],
        out_specs=pl.BlockSpec((t_rows, width), lambda b, t: (b * nt + t, 0)),
        out_shape=jax.ShapeDtypeStruct((r, width), BF16),
        scratch_shapes=[
            pltpu.VMEM((t_rows + SC_HIST, width), F32),
            pltpu.VMEM((t_rows + SC_HIST, width), F32),
            pltpu.VMEM((t_rows + SC_HIST, width), F32),
            pltpu.VMEM((nh, HEAD_DIM, HEAD_DIM), F32),
        ],
        compiler_params=_params(("arbitrary", "arbitrary")),
        name="gated_deltanet",
    )(p, p, p, p, bd_col, bd_row3, sc_w, a_log.reshape(1, nh), dt_bias.reshape(1, nh),
      a_log.reshape(nh, 1), dt_bias.reshape(nh, 1), norm_g.reshape(1, HEAD_DIM))


def _first_argmax(vals, iota, n):
    m = jnp.max(vals, axis=-1, keepdims=True)
    idx = jnp.min(jnp.where(vals == m, iota, n), axis=-1, keepdims=True)
    return m, idx


def _outproj_router_kernel(yc_ref, yd_ref, h_ref, wout_hbm, g_ref, w_ref, b_ref, h2_ref, u_ref, route_ref, cnt_ref,
                           carry_sc, w16, stage, sems, *, layer):
    kw = yc_ref.shape[1]
    rows = stage.shape[1]
    n_chunk = w16.shape[0] // rows

    @pl.when(pl.program_id(0) == 0)
    def _():
        carry_sc[...] = jnp.zeros(carry_sc.shape, F32)

        def chunk_copy(c):
            return pltpu.make_async_copy(wout_hbm.at[layer, pl.ds(c * rows, rows)], stage.at[c % 2], sems.at[c % 2])

        chunk_copy(0).start()
        for c in range(n_chunk):
            if c + 1 < n_chunk:
                chunk_copy(c + 1).start()
            chunk_copy(c).wait()
            w16[c * rows:(c + 1) * rows, :] = stage[c % 2].astype(BF16)

    x = h_ref[...] + (_dot(yc_ref[...], w16[0:kw, :]) + _dot(yd_ref[...], w16[kw:2 * kw, :]))
    h2_ref[...] = x
    ms = jnp.mean(x * x, axis=-1, keepdims=True)
    u = x * lax.rsqrt(ms + EPS) * g_ref[...]
    w = w_ref[...]
    nl = b_ref.shape[1]
    uh = u.astype(BF16)
    ul = (u - uh.astype(F32)).astype(BF16)
    u_ref[...] = _pack_bf16_pairs(uh)
    w_hi = w.astype(BF16).astype(F32)
    lane = lax.broadcasted_iota(jnp.int32, w.shape, 1)
    w_hl = jnp.where(lane < ROUTE_LO_LANE, w_hi, w - w_hi).astype(BF16)
    by_hi = _dot(uh, w_hl)
    by_lo = _dot(ul, w_hl)
    logits = by_hi[:, 0:nl] + (by_hi[:, ROUTE_LO_LANE:ROUTE_LO_LANE + nl] + by_lo[:, 0:nl]) + b_ref[...]
    tm = x.shape[0]
    glog = logits[:, 0:N_GROUPS]
    elog = logits[:, N_GROUPS:N_GROUPS + N_EXPERTS]
    gi = lax.broadcasted_iota(jnp.int32, (tm, N_GROUPS), 1)
    gmax, gsel = _first_argmax(glog, gi, N_GROUPS)
    p_group = 1.0 / jnp.sum(jnp.exp(glog - gmax), axis=-1, keepdims=True)
    ei = lax.broadcasted_iota(jnp.int32, (tm, N_EXPERTS), 1)
    in_group = (ei >= gsel * EXPERTS_PER_GROUP) & (ei < (gsel + 1) * EXPERTS_PER_GROUP)
    neg = jnp.float32(-jnp.inf)
    cand = jnp.where(in_group, elog, neg)
    m1, i1 = _first_argmax(cand, ei, N_EXPERTS)
    cand2 = jnp.where(ei == i1, neg, cand)
    m2, i2 = _first_argmax(cand2, ei, N_EXPERTS)
    e2 = jnp.exp(m2 - m1)
    w1 = p_group / (1.0 + e2)
    w2 = p_group * e2 / (1.0 + e2)
    oh1 = (ei == i1).astype(F32)
    oh2 = (ei == i2).astype(F32)
    oh = oh1 + oh2
    ri = lax.broadcasted_iota(jnp.int32, (tm, tm), 0)
    ci = lax.broadcasted_iota(jnp.int32, (tm, tm), 1)
    before = _dot((ri > ci).astype(BF16), oh.astype(BF16)) + carry_sc[...]
    r1 = jnp.sum(before * oh1, axis=-1, keepdims=True)
    r2 = jnp.sum(before * oh2, axis=-1, keepdims=True)
    carry_sc[...] += jnp.sum(oh, axis=0, keepdims=True)
    cnt_ref[...] = carry_sc[...]
    li = lax.broadcasted_iota(jnp.int32, (tm, ROUTE_COLS), 1)
    rec = jnp.zeros((tm, ROUTE_COLS), F32)
    for k, col in enumerate((i1.astype(F32), i2.astype(F32), r1, r2, w1, w2)):
        rec = jnp.where(li == k, col, rec)
    route_ref[...] = rec


def _outproj_router(y_conv, y_delta, w_out, layer, h, norm_g, w_group, b_group, w_router, b_router):
    r, d = h.shape
    kw = y_conv.shape[1]
    tm = _divisor_tile(r, ROUTER_TM, 16)
    w = jnp.concatenate([w_group, w_router], axis=1)
    b = jnp.concatenate([b_group, b_router]).reshape(1, -1)
    nl = w.shape[1]
    assert nl <= ROUTE_LO_LANE
    gap = jnp.zeros((d, ROUTE_LO_LANE - nl), w.dtype)
    w = jnp.concatenate([w, gap, w, gap], axis=1)
    stage_rows = _divisor_tile(2 * kw, WOUT_STAGE_ROWS, SUBLANES)
    kern = functools.partial(_outproj_router_kernel, layer=layer)
    rowblk = lambda width: pl.BlockSpec((tm, width), lambda i: (i, 0))
    const = lambda shape: pl.BlockSpec(shape, lambda i: (0, 0))
    return pl.pallas_call(
        kern,
        grid=(r // tm,),
        in_specs=[rowblk(kw), rowblk(kw), rowblk(d), pl.BlockSpec(memory_space=pl.ANY),
                  const((1, d)), const((d, 2 * ROUTE_LO_LANE)), const((1, nl))],
        out_specs=[rowblk(d), rowblk(d // 2), rowblk(ROUTE_COLS), const((1, N_EXPERTS))],
        out_shape=[
            jax.ShapeDtypeStruct((r, d), F32),
            jax.ShapeDtypeStruct((r, d // 2), jnp.uint32),
            jax.ShapeDtypeStruct((r, ROUTE_COLS), F32),
            jax.ShapeDtypeStruct((1, N_EXPERTS), F32),
        ],
        scratch_shapes=[
            pltpu.VMEM((1, N_EXPERTS), F32),
            pltpu.VMEM((2 * kw, d), BF16),
            pltpu.VMEM((2, stage_rows, d), F32),
            pltpu.SemaphoreType.DMA((2,)),
        ],
        compiler_params=_params(("arbitrary",)),
        name="outproj_router",
    )(y_conv, y_delta, h, w_out, norm_g.reshape(1, d), w, b)


def _dispatch_plan(route, cnt, tm, n_tiles):
    cnt = cnt[0].astype(jnp.int32)
    padded = ((cnt + tm - 1) // tm) * tm
    ends = jnp.cumsum(padded)
    off = ends - padded
    onehot = (route[:, 0:TOP_K, None] == jnp.arange(N_EXPERTS, dtype=F32)).astype(F32)
    pos = jnp.einsum("rke,e->rk", onehot, off.astype(F32), precision=HI) + route[:, TOP_K:2 * TOP_K]
    pos = pos.astype(jnp.int32)
    pos0, pos1 = pos[:, 0], pos[:, 1]
    n_used = ends[-1] // tm
    tiles = jnp.arange(n_tiles, dtype=jnp.int32)
    tile_e = jnp.sum((tiles[:, None] * tm >= ends[None, :]).astype(jnp.int32), axis=1)
    tile_e = jnp.minimum(jnp.where(tiles < n_used, tile_e, tile_e[n_used - 1]), N_EXPERTS - 1)
    experts = jnp.arange(N_EXPERTS, dtype=jnp.int32)
    has_rows = cnt > 0
    slot_tab = (jnp.cumsum(has_rows.astype(jnp.int32)) - 1) % 2
    later = jnp.where(has_rows[None, :] & (experts[None, :] > experts[:, None]), experts[None, :], N_EXPERTS)
    next_tab = jnp.min(later, axis=1)
    next_tab = jnp.where(next_tab == N_EXPERTS, -1, next_tab)
    sel = (tile_e[:, None] == experts[None, :]).astype(jnp.int32)
    plan = dict(tile_e=tile_e, slot=jnp.sum(sel * slot_tab[None, :], axis=1), next_e=jnp.sum(sel * next_tab[None, :], axis=1),
                n_used=n_used.reshape(1), pad_start=off + cnt, pad_end=ends)
    return pos0, pos1, plan


def _dispatch_kernel(pos0_ref, pos1_ref, ps_ref, pe_ref, nu_ref, u_ref, xs_ref, zbuf, sem, zsem, *, t_rows, tm, n_tiles,
                     n_steps):
    i = pl.program_id(0)

    def zero_copy(start, n):
        return pltpu.make_async_copy(zbuf.at[pl.ds(0, n)], xs_ref.at[pl.ds(start, n)], zsem)

    def pad_pieces(e):
        start, end = ps_ref[e], pe_ref[e]
        aligned = jnp.minimum((start + (SUBLANES - 1)) & (-SUBLANES), end)
        pieces = [(start + k < aligned, start + k, 1) for k in range(SUBLANES - 1)]
        rem = end - aligned
        at = aligned
        n = tm // 2
        while n >= SUBLANES:
            pieces.append(((rem & n) != 0, pl.multiple_of(at, SUBLANES), n))
            at = at + (rem & n)
            n //= 2
        return pieces

    def for_each_piece(action):
        def pad_body(e, c):
            for take, start, n in pad_pieces(e):
                pl.when(take)(functools.partial(action, start, n))
            return c

        def tail_body(j, c):
            pl.when(j >= nu_ref[0])(functools.partial(action, pl.multiple_of(j * tm, SUBLANES), tm))
            return c

        lax.fori_loop(0, N_EXPERTS, pad_body, 0)
        lax.fori_loop(0, n_tiles, tail_body, 0)

    @pl.when(i == 0)
    def _():
        zbuf[...] = jnp.zeros(zbuf.shape, zbuf.dtype)
        for_each_piece(lambda start, n: zero_copy(start, n).start())

    base = i * t_rows

    def body(r, c):
        src = u_ref.at[pl.ds(r, 1)]
        pltpu.make_async_copy(src, xs_ref.at[pl.ds(pos0_ref[base + r], 1)], sem).start()
        pltpu.make_async_copy(src, xs_ref.at[pl.ds(pos1_ref[base + r], 1)], sem).start()
        return c

    lax.fori_loop(0, t_rows, body, 0, unroll=DMA_UNROLL)
    for _ in range(2):
        pltpu.make_async_copy(u_ref, xs_ref.at[pl.ds(0, t_rows)], sem).wait()

    @pl.when(i == n_steps - 1)
    def _():
        for_each_piece(lambda start, n: zero_copy(start, n).wait())


def _dispatch(u, pos0, pos1, plan, n_slots, tm):
    r, d = u.shape
    t_rows = _divisor_tile(r, DISPATCH_T, SUBLANES)
    n_tiles = n_slots // tm
    assert tm & (tm - 1) == 0 and tm >= 2 * SUBLANES
    kern = functools.partial(_dispatch_kernel, t_rows=t_rows, tm=tm, n_tiles=n_tiles, n_steps=r // t_rows)
    return pl.pallas_call(
        kern,
        grid_spec=pltpu.PrefetchScalarGridSpec(
            num_scalar_prefetch=5,
            grid=(r // t_rows,),
            in_specs=[pl.BlockSpec((t_rows, d), lambda i, *_: (i, 0))],
            out_specs=pl.BlockSpec(memory_space=pl.ANY),
            scratch_shapes=[pltpu.VMEM((tm, d), u.dtype), pltpu.SemaphoreType.DMA(()), pltpu.SemaphoreType.DMA(())],
        ),
        out_shape=jax.ShapeDtypeStruct((n_slots, d), u.dtype),
        compiler_params=_params(("arbitrary",)),
        name="moe_dispatch",
    )(pos0, pos1, plan["pad_start"], plan["pad_end"], plan["n_used"], u)


def _expert_kernel(te_ref, slot_ref, nxt_ref, nu_ref, x_ref, wg_hbm, wu_hbm, wd_hbm, y_ref,
                   wg_f, wu_f, wd_f, wg_b, wu_b, wd_b, sems, *, layer):
    i = pl.program_id(0)
    e = te_ref[i]
    slot = slot_ref[i]
    used = i < nu_ref[0]
    first_of_expert = used & ((i == 0) | (te_ref[jnp.maximum(i - 1, 0)] != e))

    def weight_copies(expert, s):
        return [pltpu.make_async_copy(src.at[layer, expert], dst.at[s], sems.at[s])
                for src, dst in ((wg_hbm, wg_f), (wu_hbm, wu_f), (wd_hbm, wd_f))]

    @pl.when(i == 0)
    def _():
        for c in weight_copies(e, slot):
            c.start()

    @pl.when(first_of_expert)
    def _():
        for c in weight_copies(e, slot):
            c.wait()
        nxt = nxt_ref[i]

        @pl.when(nxt >= 0)
        def _():
            for c in weight_copies(nxt, 1 - slot):
                c.start()

        wg_b[...] = wg_f[slot].astype(BF16)
        wu_b[...] = wu_f[slot].astype(BF16)
        wd_b[...] = wd_f[slot].astype(BF16)

    @pl.when(used)
    def _():
        xa, xb = _unpack_bf16_pairs(x_ref[...])
        half = xa.shape[1]
        hg = _dot(xa, wg_b[0:half, :]) + _dot(xb, wg_b[half:2 * half, :])
        hu = _dot(xa, wu_b[0:half, :]) + _dot(xb, wu_b[half:2 * half, :])
        hid = (_silu(hg) * hu).astype(BF16)
        y_ref[...] = _pack_bf16_pairs(_dot(hid, wd_b[...]).astype(BF16))

    @pl.when(jnp.logical_not(used))
    def _():
        y_ref[...] = jnp.zeros(y_ref.shape, y_ref.dtype)


def _experts(xs, plan, w_gate, w_up, w_down, layer, tm):
    n_slots = xs.shape[0]
    _, _, d, f = w_gate.shape
    assert xs.shape[1] * 2 == d
    n_tiles = n_slots // tm
    used_row = lambda i, te, sl, nx, nu: (jnp.minimum(i, nu[0] - 1), 0)
    hbm = pl.BlockSpec(memory_space=pl.ANY)
    kern = functools.partial(_expert_kernel, layer=layer)
    return pl.pallas_call(
        kern,
        grid_spec=pltpu.PrefetchScalarGridSpec(
            num_scalar_prefetch=4,
            grid=(n_tiles,),
            in_specs=[pl.BlockSpec((tm, d // 2), used_row), hbm, hbm, hbm],
            out_specs=pl.BlockSpec((tm, d // 2), lambda i, te, sl, nx, nu: (i, 0)),
            scratch_shapes=[
                pltpu.VMEM((2, d, f), F32), pltpu.VMEM((2, d, f), F32), pltpu.VMEM((2, f, d), F32),
                pltpu.VMEM((d, f), BF16), pltpu.VMEM((d, f), BF16), pltpu.VMEM((f, d), BF16),
                pltpu.SemaphoreType.DMA((2,)),
            ],
        ),
        out_shape=jax.ShapeDtypeStruct((n_slots, d // 2), jnp.uint32),
        compiler_params=_params(("arbitrary",)),
        name="moe_experts",
    )(plan["tile_e"], plan["slot"], plan["next_e"], plan["n_used"], xs, w_gate, w_up, w_down)


def _combine_kernel(pos0_ref, pos1_ref, h_ref, route_ref, ys_ref, o_ref, buf0, buf1, sems, *, t_rows, n_steps):
    i = pl.program_id(0)

    def gather_tile(step, slot):
        base = step * t_rows

        def body(r, c):
            pltpu.make_async_copy(ys_ref.at[pl.ds(pos0_ref[base + r], 1)], buf0.at[slot, pl.ds(r, 1)], sems.at[slot]).start()
            pltpu.make_async_copy(ys_ref.at[pl.ds(pos1_ref[base + r], 1)], buf1.at[slot, pl.ds(r, 1)], sems.at[slot]).start()
            return c

        lax.fori_loop(0, t_rows, body, 0, unroll=DMA_UNROLL)

    @pl.when(i == 0)
    def _():
        gather_tile(0, 0)

    @pl.when(i + 1 < n_steps)
    def _():
        gather_tile(i + 1, (i + 1) % 2)

    slot = i % 2
    for buf in (buf0, buf1):
        pltpu.make_async_copy(ys_ref.at[pl.ds(0, t_rows)], buf.at[slot], sems.at[slot]).wait()
    g = route_ref[...]
    lo0, hi0 = _unpack_bf16_pairs(buf0[slot], F32)
    lo1, hi1 = _unpack_bf16_pairs(buf1[slot], F32)
    half = lo0.shape[1]
    o_ref[:, 0:half] = h_ref[:, 0:half] + g[:, 4:5] * lo0 + g[:, 5:6] * lo1
    o_ref[:, half:2 * half] = h_ref[:, half:2 * half] + g[:, 4:5] * hi0 + g[:, 5:6] * hi1


def _combine(h, route, ys, pos0, pos1):
    r, d = h.shape
    t_rows = _divisor_tile(r, COMBINE_T, SUBLANES)
    n_steps = r // t_rows
    kern = functools.partial(_combine_kernel, t_rows=t_rows, n_steps=n_steps)
    return pl.pallas_call(
        kern,
        grid_spec=pltpu.PrefetchScalarGridSpec(
            num_scalar_prefetch=2,
            grid=(n_steps,),
            in_specs=[
                pl.BlockSpec((t_rows, d), lambda i, p0, p1: (i, 0)),
                pl.BlockSpec((t_rows, ROUTE_COLS), lambda i, p0, p1: (i, 0)),
                pl.BlockSpec(memory_space=pl.ANY),
            ],
            out_specs=pl.BlockSpec((t_rows, d), lambda i, p0, p1: (i, 0)),
            scratch_shapes=[pltpu.VMEM((2, t_rows, d // 2), ys.dtype), pltpu.VMEM((2, t_rows, d // 2), ys.dtype),
                            pltpu.SemaphoreType.DMA((2,))],
        ),
        out_shape=jax.ShapeDtypeStruct((r, d), F32),
        compiler_params=_params(("arbitrary",)),
        name="moe_combine",
    )(pos0, pos1, h, route, ys)


def _routed_experts(h, u, route, cnt, w_gate, w_up, w_down, layer):
    r = h.shape[0]
    tm = EXPERT_TM
    n_tiles = -(-(TOP_K * r + N_EXPERTS * (tm - 1)) // tm)
    pos0, pos1, plan = _dispatch_plan(route, cnt, tm, n_tiles)
    xs = _dispatch(u, pos0, pos1, plan, n_tiles * tm, tm)
    ys = _experts(xs, plan, w_gate, w_up, w_down, layer, tm)
    return _combine(h, route, ys, pos0, pos1)


def _final_kernel(h_ref, g_ref, o_ref):
    x = h_ref[...]
    ms = jnp.mean(x * x, axis=-1, keepdims=True)
    o_ref[0] = x * lax.rsqrt(ms + EPS) * g_ref[...]


def _final_norm(h, norm_g, nb, lp, seq, skip):
    d = h.shape[1]
    t_rows = _divisor_tile(seq, FINAL_T, SUBLANES)
    assert skip % SUBLANES == 0 and lp % SUBLANES == 0
    first_row = lambda b, t: (pl.multiple_of(b * lp + skip + t * t_rows, SUBLANES), 0)
    return pl.pallas_call(
        _final_kernel,
        grid=(nb, seq // t_rows),
        in_specs=[
            pl.BlockSpec((pl.Element(t_rows), pl.Element(d)), first_row),
            pl.BlockSpec((1, d), lambda b, t: (0, 0)),
        ],
        out_specs=pl.BlockSpec((1, t_rows, d), lambda b, t: (b, t, 0)),
        out_shape=jax.ShapeDtypeStruct((nb, seq, d), F32),
        compiler_params=_params(("arbitrary", "arbitrary")),
        name="final_norm",
    )(h, norm_g.reshape(1, d))


def kernel(x, meta, attn_norm, w_in, conv_dw_w, conv_dw_b, conv_ln_g, conv_ln_b, short_conv_w, a_log, dt_bias,
           delta_norm_g, w_out, ffn_norm, w_group, b_group, w_router, b_router, w_gate, w_up, w_down, final_norm):
    nb, seq, d = x.shape
    depth = w_in.shape[0]
    conv_w = conv_dw_w.shape[2]
    delta_w = short_conv_w.shape[2] // 3
    nh = delta_w // HEAD_DIM
    n_main = 2 * conv_w + 4 * delta_w
    assert w_in.shape[2] == n_main + 2 * nh and conv_w == delta_w
    ln = N_META + seq
    pad = (-ln) % CHUNK
    lp = ln + pad
    skip = pad + N_META

    meta_b = jnp.broadcast_to(meta[None].astype(x.dtype), (nb, N_META, d))
    h = jnp.concatenate([jnp.zeros((nb, pad, d), x.dtype), meta_b, x], axis=1).reshape(nb * lp, d)

    w_in_t = jnp.swapaxes(w_in, 1, 2)
    for l in range(depth):
        p, bd_col, bd_row = _inproj(h, attn_norm[l], w_in_t, l, n_main, lp, pad, nb)
        bd_row3 = bd_row.reshape(2 * nh, nb * lp // CHUNK, CHUNK).transpose(1, 0, 2)
        y_conv = _conformer_conv(p, conv_dw_w[l], conv_dw_b[l], conv_ln_g[l], conv_ln_b[l], nb, lp, conv_w)
        y_delta = _gated_deltanet(p, bd_col, bd_row3, short_conv_w[l], a_log[l], dt_bias[l], delta_norm_g[l],
                                  nb, lp, pad, delta_w, 2 * conv_w)
        h, u, route, cnt = _outproj_router(y_conv, y_delta, w_out, l, h, ffn_norm[l], w_group[l], b_group[l],
                                           w_router[l], b_router[l])
        h = _routed_experts(h, u, route, cnt, w_gate, w_up, w_down, l)
    return _final_norm(h, final_norm, nb, lp, seq, skip)
```

```python
import functools

import jax
import jax.numpy as jnp
from jax import lax
from jax.experimental import pallas as pl
from jax.experimental.pallas import tpu as pltpu

F32 = jnp.float32
BF16 = jnp.bfloat16
HI = lax.Precision.HIGHEST

EPS = 1e-6
MASKED_EXPONENT = -1e30
CHUNK = 64
N_META = 16
CONV_GROUPS = 8
CONV_KERNEL = 31
HEAD_DIM = 128
SHORT_CONV = 4
N_GROUPS = 4
EXPERTS_PER_GROUP = 8
N_EXPERTS = N_GROUPS * EXPERTS_PER_GROUP
TOP_K = 2
ROUTE_COLS = 8
ROUTE_LO_LANE = 64
LANES = 128
SUBLANES = 8
CONV_HIST = 32
SC_HIST = 8
VMEM_LIMIT = 56 * 1024 * 1024

INPROJ_TM, INPROJ_TN = 1664, 512
CONV_CHUNKS = 13
DELTA_CHUNKS = 5
DELTA_INTERLEAVE = 3
WOUT_STAGE_ROWS = 512
ROUTER_TM = 640
EXPERT_TM = 256
EXPERT_WEIGHT_BUFS = 3
DISPATCH_T = 640
COMBINE_T = 640
DMA_UNROLL = 8
FINAL_T = 512


def _divisor_tile(n, cap, mult):
    best = None
    for t in range(mult, min(n, cap) + 1, mult):
        if n % t == 0:
            best = t
    if best is None:
        raise ValueError(f"no tile for n={n} cap={cap} mult={mult}")
    return best


def _params(sem):
    return pltpu.CompilerParams(dimension_semantics=sem, vmem_limit_bytes=VMEM_LIMIT)


def _dot(a, b, precision=None):
    return jnp.dot(a, b, preferred_element_type=F32, precision=precision)


def _dot_nt(a, b, precision=None):
    return lax.dot_general(a, b, (((1,), (1,)), ((), ())), preferred_element_type=F32, precision=precision)


def _silu(x):
    return x * jax.nn.sigmoid(x)


def _softplus(x):
    return jnp.maximum(x, 0.0) + jnp.log1p(jnp.exp(-jnp.abs(x)))


def _dot_split(a, b, n_parts, split_lhs):
    x = a if split_lhs else b
    acc = None
    for _ in range(n_parts):
        piece = x.astype(BF16)
        term = _dot(piece, b) if split_lhs else _dot(a, piece)
        acc = term if acc is None else acc + term
        x = x - piece.astype(F32)
    return acc


def _pack_bf16_pairs(x):
    n = x.shape[1] // 2
    lo = lax.bitcast_convert_type(x[:, 0:n].astype(F32), jnp.uint32)
    hi = lax.bitcast_convert_type(x[:, n:2 * n].astype(F32), jnp.uint32)
    return (lo >> 16) | hi


def _unpack_bf16_pairs(w, dtype=BF16):
    lo = lax.bitcast_convert_type(w << 16, F32).astype(dtype)
    hi = lax.bitcast_convert_type(w & jnp.uint32(0xFFFF0000), F32).astype(dtype)
    return lo, hi


def _causal_taps(win, tap_w, n_taps, first_tap):
    rows = win.shape[0]
    acc = jnp.zeros((CHUNK,) + win.shape[1:], F32)
    for res in range(SUBLANES):
        offs = [o for o in range(first_tap, first_tap + n_taps) if o % SUBLANES == res]
        if not offs:
            continue
        shifted = pltpu.roll(win, rows - res, axis=0) if res else win
        for o in offs:
            assert o + CHUNK <= rows
            a = o - res
            acc = acc + tap_w(o - first_tap) * shifted[a:a + CHUNK, :]
    return acc


def _inproj_kernel(h_ref, g_ref, w_ref, wbd_ref, p_ref, bdc_ref, bdr_ref, u_sc, *, tm, rb, lp, pad, nb):
    i = pl.program_id(0)
    j = pl.program_id(1)

    @pl.when(j == 0)
    def _():
        wbd = wbd_ref[...].astype(BF16)
        for blk in range(tm // rb):
            rows = slice(blk * rb, (blk + 1) * rb)
            x = h_ref[rows, :]
            ms = jnp.mean(x * x, axis=-1, keepdims=True)
            u = x * lax.rsqrt(ms + EPS) * g_ref[...]
            row = i * tm + blk * rb + lax.broadcasted_iota(jnp.int32, (rb, 1), 0)
            valid = (row >= pad) & (row < lp)
            for b in range(1, nb):
                valid = valid | ((row >= b * lp + pad) & (row < (b + 1) * lp))
            ub = jnp.where(valid, u, 0.0).astype(BF16)
            u_sc[rows, :] = ub
            bdc_ref[rows, :] = _dot_nt(ub, wbd)
            bdr_ref[:, rows] = _dot_nt(wbd, ub)

    p_ref[...] = _dot_nt(u_sc[...], w_ref[...].astype(BF16))


def _inproj(h, norm_g, w_in_t, layer, n_main, lp, pad, nb):
    r, d = h.shape
    n_bd = w_in_t.shape[1] - n_main
    tm = _divisor_tile(r, INPROJ_TM, LANES)
    tn = _divisor_tile(n_main, INPROJ_TN, LANES)
    assert n_main % n_bd == 0 and n_bd % SUBLANES == 0
    kern = functools.partial(_inproj_kernel, tm=tm, rb=LANES, lp=lp, pad=pad, nb=nb)
    return pl.pallas_call(
        kern,
        grid=(r // tm, n_main // tn),
        in_specs=[
            pl.BlockSpec((tm, d), lambda i, j: (i, 0)),
            pl.BlockSpec((1, d), lambda i, j: (0, 0)),
            pl.BlockSpec((None, tn, d), lambda i, j: (layer, j, 0)),
            pl.BlockSpec((None, n_bd, d), lambda i, j: (layer, n_main // n_bd, 0)),
        ],
        out_specs=[
            pl.BlockSpec((tm, tn), lambda i, j: (i, j)),
            pl.BlockSpec((tm, n_bd), lambda i, j: (i, 0)),
            pl.BlockSpec((n_bd, tm), lambda i, j: (0, i)),
        ],
        out_shape=[
            jax.ShapeDtypeStruct((r, n_main), F32),
            jax.ShapeDtypeStruct((r, n_bd), F32),
            jax.ShapeDtypeStruct((n_bd, r), F32),
        ],
        scratch_shapes=[pltpu.VMEM((tm, d), BF16)],
        compiler_params=_params(("arbitrary", "arbitrary")),
        name="inproj",
    )(h, norm_g.reshape(1, d), w_in_t, w_in_t)


def _conv_kernel(a_ref, b_ref, w_ref, bias_ref, lg_ref, lb_ref, o_ref, ybuf, *, t_rows, width):
    t = pl.program_id(1)
    n_blk = t_rows // CHUNK

    @pl.when(t == 0)
    def _():
        ybuf[0:CONV_HIST, :] = jnp.zeros((CONV_HIST, width), F32)

    @pl.when(t > 0)
    def _():
        ybuf[0:CONV_HIST, :] = ybuf[t_rows:t_rows + CONV_HIST, :]

    def glu_body(r, c):
        r0 = pl.multiple_of(r * CHUNK, CHUNK)
        a = a_ref[pl.ds(r0, CHUNK), :]
        g = b_ref[pl.ds(r0, CHUNK), :]
        ybuf[pl.ds(CONV_HIST + r0, CHUNK), :] = a * jax.nn.sigmoid(g)
        return c

    lax.fori_loop(0, n_blk, glu_body, 0)

    first_tap = CONV_HIST - (CONV_KERNEL - 1)

    def body(r, c):
        r0 = pl.multiple_of(r * CHUNK, CHUNK)
        for gi in range(width // LANES):
            ls = slice(gi * LANES, (gi + 1) * LANES)
            win = ybuf[pl.ds(r0, CHUNK + CONV_HIST), ls]
            acc = _causal_taps(win, lambda k: w_ref[k:k + 1, ls], CONV_KERNEL, first_tap)
            y = acc + bias_ref[:, ls]
            mu = jnp.mean(y, axis=-1, keepdims=True)
            dlt = y - mu
            var = jnp.mean(dlt * dlt, axis=-1, keepdims=True)
            yn = dlt * lax.rsqrt(var + EPS) * lg_ref[:, ls] + lb_ref[:, ls]
            o_ref[pl.ds(r0, CHUNK), ls] = _silu(yn).astype(BF16)
        return c

    lax.fori_loop(0, n_blk, body, 0)


def _conformer_conv(p, w_dw, b_dw, ln_g, ln_b, nb, lp, width):
    r = p.shape[0]
    nch = lp // CHUNK
    t_rows = CHUNK * _divisor_tile(nch, CONV_CHUNKS, 1)
    nt = lp // t_rows
    assert width // LANES == CONV_GROUPS
    kern = functools.partial(_conv_kernel, t_rows=t_rows, width=width)
    vec = lambda b, t: (0, 0)
    return pl.pallas_call(
        kern,
        grid=(nb, nt),
        in_specs=[
            pl.BlockSpec((t_rows, width), lambda b, t: (b * nt + t, 0)),
            pl.BlockSpec((t_rows, width), lambda b, t: (b * nt + t, 1)),
            pl.BlockSpec((CONV_KERNEL, width), vec),
            pl.BlockSpec((1, width), vec),
            pl.BlockSpec((1, width), vec),
            pl.BlockSpec((1, width), vec),
        ],
        out_specs=pl.BlockSpec((t_rows, width), lambda b, t: (b * nt + t, 0)),
        out_shape=jax.ShapeDtypeStruct((r, width), BF16),
        scratch_shapes=[pltpu.VMEM((t_rows + CONV_HIST, width), F32)],
        compiler_params=_params(("arbitrary", "arbitrary")),
        name="conformer_conv",
    )(p, p, w_dw, b_dw.reshape(1, width), ln_g.reshape(1, width), ln_b.reshape(1, width))


def _delta_kernel(q_ref, k_ref, v_ref, z_ref, bdc_ref, bdr_ref, scw_ref, alr_ref, dtr_ref, alc_ref, dtc_ref,
                  ng_ref, o_ref, qbuf, kbuf, vbuf, s_sc, *, t_rows, pad, width, nh):
    t = pl.program_id(1)
    n_blk = t_rows // CHUNK
    bufs = (qbuf, kbuf, vbuf)
    srcs = (q_ref, k_ref, v_ref)

    @pl.when(t == 0)
    def _():
        s_sc[...] = jnp.zeros(s_sc.shape, F32)
        for buf in bufs:
            buf[0:SC_HIST, :] = jnp.zeros((SC_HIST, width), F32)

    @pl.when(t > 0)
    def _():
        for buf in bufs:
            buf[0:SC_HIST, :] = buf[t_rows:t_rows + SC_HIST, :]

    def copy_body(r, c):
        r0 = pl.multiple_of(r * CHUNK, CHUNK)
        for buf, src in zip(bufs, srcs):
            buf[pl.ds(SC_HIST + r0, CHUNK), :] = src[pl.ds(r0, CHUNK), :]
        return c

    lax.fori_loop(0, n_blk, copy_body, 0)

    ri = lax.broadcasted_iota(jnp.int32, (CHUNK, CHUNK), 0)
    ci = lax.broadcasted_iota(jnp.int32, (CHUNK, CHUNK), 1)
    incl = ri >= ci
    strict = ri > ci
    tril = incl.astype(BF16)
    triu = (ri <= ci).astype(BF16)
    eye = (ri == ci).astype(F32)

    def head_expand(lanes):
        eh = lax.broadcasted_iota(jnp.int32, (nh, nh * lanes), 0)
        ec = lax.broadcasted_iota(jnp.int32, (nh, nh * lanes), 1)
        return ((ec >= eh * lanes) & (ec < (eh + 1) * lanes)).astype(BF16)

    expand_c = head_expand(CHUNK)
    expand_d = head_expand(HEAD_DIM)
    first_tap = SC_HIST - (SHORT_CONV - 1)
    q_scale = HEAD_DIM ** -0.5
    heads = range(nh)

    def short_conv(buf, part, r0):
        win = buf[pl.ds(r0, CHUNK + SC_HIST), :]
        acc = _causal_taps(win, lambda k: scw_ref[k:k + 1, part * width:(part + 1) * width], SHORT_CONV, first_tap)
        return _silu(acc)

    hd = [slice(h * HEAD_DIM, (h + 1) * HEAD_DIM) for h in heads]
    hc = [slice(h * CHUNK, (h + 1) * CHUNK) for h in heads]

    def chunk_inputs(c):
        r0 = pl.multiple_of(c * CHUNK, CHUNK)
        lrow = t * t_rows + r0 + lax.broadcasted_iota(jnp.int32, (CHUNK, 1), 0)
        lcol = t * t_rows + r0 + lax.broadcasted_iota(jnp.int32, (1, CHUNK), 1)
        bl = bdc_ref[pl.ds(r0, CHUNK), :]
        br = bdr_ref[c]
        beta_col = jnp.where(lrow >= pad, jax.nn.sigmoid(bl[:, 0:nh]), 0.0)
        beta_row = jnp.where(lcol >= pad, jax.nn.sigmoid(br[0:nh, :]), 0.0)
        g_col = jnp.where(lrow >= pad, -jnp.exp(alr_ref[...]) * _softplus(bl[:, nh:2 * nh] + dtr_ref[...]), 0.0)
        g_row = jnp.where(lcol >= pad, -jnp.exp(alc_ref[...]) * _softplus(br[nh:2 * nh, :] + dtc_ref[...]), 0.0)
        gc_col = _dot_split(tril, g_col, 3, split_lhs=False)
        gc_row = _dot_split(g_row, triu, 3, split_lhs=True)
        gc_x = _dot_split(gc_col, expand_c, 3, split_lhs=True)
        qc = short_conv(qbuf, 0, r0)
        kc = short_conv(kbuf, 1, r0)
        vc = short_conv(vbuf, 2, r0)
        qn = [qc[:, s] * lax.rsqrt(jnp.sum(qc[:, s] * qc[:, s], axis=-1, keepdims=True) + EPS) * q_scale for s in hd]
        kn = [kc[:, s] * lax.rsqrt(jnp.sum(kc[:, s] * kc[:, s], axis=-1, keepdims=True) + EPS) for s in hd]
        return dict(
            r0=r0, beta_row=beta_row, eg_row=jnp.exp(gc_row), ekd_row=jnp.exp(gc_row[:, CHUNK - 1:CHUNK] - gc_row),
            beta_x=_dot_split(beta_col, expand_c, 2, split_lhs=True),
            eg_x=_dot_split(jnp.exp(gc_col), expand_d, 2, split_lhs=True),
            qn=qn, kn=kn, k16=[k.astype(BF16) for k in kn], v16=[vc[:, s].astype(BF16) for s in hd],
            decay=[jnp.exp(jnp.where(incl, gc_x[:, hc[h]] - gc_row[h:h + 1, :], MASKED_EXPONENT)) for h in heads],
            zc=z_ref[pl.ds(r0, CHUNK), :])

    def process(chunks):
        cin = [chunk_inputs(c) for c in chunks]
        pairs = [(ci, h) for ci in range(len(chunks)) for h in heads]
        kq = [_dot_nt(jnp.concatenate([cin[ci]["k16"][h], cin[ci]["qn"][h].astype(BF16)], axis=0), cin[ci]["k16"][h])
              for ci, h in pairs]
        xp = [jnp.where(strict, -(kq[n][0:CHUNK] * cin[ci]["beta_x"][:, hc[h]] * cin[ci]["decay"][h]), 0.0)
              for n, (ci, h) in enumerate(pairs)]
        intra = [kq[n][CHUNK:2 * CHUNK] * cin[ci]["decay"][h] for n, (ci, h) in enumerate(pairs)]
        ainv = [eye + x for x in xp]
        n_sq = 1
        while 2 * n_sq < CHUNK:
            xp16 = [x.astype(BF16) for x in xp]
            xp = [_dot(x, x) for x in xp16]
            ainv = [a + _dot(a.astype(BF16), x.astype(BF16)) for a, x in zip(ainv, xp)]
            n_sq *= 2
        u = [_dot((ainv[n] * cin[ci]["beta_row"][h:h + 1, :]).astype(BF16), cin[ci]["v16"][h]) for n, (ci, h) in enumerate(pairs)]
        w = [_dot((ainv[n] * (cin[ci]["beta_row"][h:h + 1, :] * cin[ci]["eg_row"][h:h + 1, :])).astype(BF16), cin[ci]["k16"][h])
             for n, (ci, h) in enumerate(pairs)]
        for ci, cc in enumerate(cin):
            base = ci * nh
            q_dec = [cc["qn"][h] * cc["eg_x"][:, hd[h]] for h in heads]
            kd_t = [cc["kn"][h].T * cc["ekd_row"][h:h + 1, :] for h in heads]
            s_old = [s_sc[h] for h in heads]
            wq_s = [_dot(jnp.concatenate([w[base + h], q_dec[h]], axis=0).astype(BF16), s_old[h].astype(BF16)) for h in heads]
            v_new = [u[base + h] - wq_s[h][0:CHUNK] for h in heads]
            iv = [_dot(jnp.concatenate([intra[base + h], kd_t[h]], axis=0).astype(BF16), v_new[h].astype(BF16)) for h in heads]
            for h in heads:
                o = wq_s[h][CHUNK:2 * CHUNK] + iv[h][0:CHUNK]
                s_sc[h] = s_old[h] * cc["eg_x"][CHUNK - 1:CHUNK, hd[h]] + iv[h][CHUNK:CHUNK + HEAD_DIM]
                on = o * lax.rsqrt(jnp.mean(o * o, axis=-1, keepdims=True) + EPS) * ng_ref[...]
                o_ref[pl.ds(cc["r0"], CHUNK), hd[h]] = (on * _silu(cc["zc"][:, hd[h]])).astype(BF16)

    n_grp = n_blk // DELTA_INTERLEAVE

    def body(g, carry):
        process([g * DELTA_INTERLEAVE + k for k in range(DELTA_INTERLEAVE)])
        return carry

    lax.fori_loop(0, n_grp, body, 0)
    if n_blk % DELTA_INTERLEAVE:
        process(list(range(n_grp * DELTA_INTERLEAVE, n_blk)))


def _gated_deltanet(p, bd_col, bd_row3, sc_w, a_log, dt_bias, norm_g, nb, lp, pad, width, col0):
    r = p.shape[0]
    nh = width // HEAD_DIM
    nch = lp // CHUNK
    cpt = _divisor_tile(nch, DELTA_CHUNKS, 1)
    t_rows = CHUNK * cpt
    nt = lp // t_rows
    cb = col0 // width
    assert col0 % width == 0
    kern = functools.partial(_delta_kernel, t_rows=t_rows, pad=pad, width=width, nh=nh)
    vec = lambda b, t: (0, 0)
    part = lambda off: pl.BlockSpec((t_rows, width), lambda b, t: (b * nt + t, cb + off))
    return pl.pallas_call(
        kern,
        grid=(nb, nt),
        in_specs=[
            part(0), part(1), part(2), part(3),
            pl.BlockSpec((t_rows, 2 * nh), lambda b, t: (b * nt + t, 0)),
            pl.BlockSpec((cpt, 2 * nh, CHUNK), lambda b, t: (b * nt + t, 0, 0)),
            pl.BlockSpec((SHORT_CONV, 3 * width), vec),
            pl.BlockSpec((1, nh), vec),
            pl.BlockSpec((1, nh), vec),
            pl.BlockSpec((nh, 1), vec),
            pl.BlockSpec((nh, 1), vec),
            pl.BlockSpec((1, HEAD_DIM), vec),
        ],
        out_specs=pl.BlockSpec((t_rows, width), lambda b, t: (b * nt + t, 0)),
        out_shape=jax.ShapeDtypeStruct((r, width), BF16),
        scratch_shapes=[
            pltpu.VMEM((t_rows + SC_HIST, width), F32),
            pltpu.VMEM((t_rows + SC_HIST, width), F32),
            pltpu.VMEM((t_rows + SC_HIST, width), F32),
            pltpu.VMEM((nh, HEAD_DIM, HEAD_DIM), F32),
        ],
        compiler_params=_params(("arbitrary", "arbitrary")),
        name="gated_deltanet",
    )(p, p, p, p, bd_col, bd_row3, sc_w, a_log.reshape(1, nh), dt_bias.reshape(1, nh),
      a_log.reshape(nh, 1), dt_bias.reshape(nh, 1), norm_g.reshape(1, HEAD_DIM))


def _first_argmax(vals, iota, n):
    m = jnp.max(vals, axis=-1, keepdims=True)
    idx = jnp.min(jnp.where(vals == m, iota, n), axis=-1, keepdims=True)
    return m, idx


def _outproj_router_kernel(yc_ref, yd_ref, h_ref, wout_hbm, g_ref, w_ref, b_ref, h2_ref, u_ref, route_ref, cnt_ref,
                           carry_sc, w16, stage, sems, *, layer):
    kw = yc_ref.shape[1]
    rows = stage.shape[1]
    n_chunk = w16.shape[0] // rows

    @pl.when(pl.program_id(0) == 0)
    def _():
        carry_sc[...] = jnp.zeros(carry_sc.shape, F32)

        def chunk_copy(c):
            return pltpu.make_async_copy(wout_hbm.at[layer, pl.ds(c * rows, rows)], stage.at[c % 2], sems.at[c % 2])

        chunk_copy(0).start()
        for c in range(n_chunk):
            if c + 1 < n_chunk:
                chunk_copy(c + 1).start()
            chunk_copy(c).wait()
            w16[c * rows:(c + 1) * rows, :] = stage[c % 2].astype(BF16)

    x = h_ref[...] + (_dot(yc_ref[...], w16[0:kw, :]) + _dot(yd_ref[...], w16[kw:2 * kw, :]))
    h2_ref[...] = x
    ms = jnp.mean(x * x, axis=-1, keepdims=True)
    u = x * lax.rsqrt(ms + EPS) * g_ref[...]
    w = w_ref[...]
    nl = b_ref.shape[1]
    uh = u.astype(BF16)
    ul = (u - uh.astype(F32)).astype(BF16)
    u_ref[...] = _pack_bf16_pairs(uh)
    w_hi = w.astype(BF16).astype(F32)
    lane = lax.broadcasted_iota(jnp.int32, w.shape, 1)
    w_hl = jnp.where(lane < ROUTE_LO_LANE, w_hi, w - w_hi).astype(BF16)
    by_hi = _dot(uh, w_hl)
    by_lo = _dot(ul, w_hl)
    logits = by_hi[:, 0:nl] + (by_hi[:, ROUTE_LO_LANE:ROUTE_LO_LANE + nl] + by_lo[:, 0:nl]) + b_ref[...]
    tm = x.shape[0]
    glog = logits[:, 0:N_GROUPS]
    elog = logits[:, N_GROUPS:N_GROUPS + N_EXPERTS]
    gi = lax.broadcasted_iota(jnp.int32, (tm, N_GROUPS), 1)
    gmax, gsel = _first_argmax(glog, gi, N_GROUPS)
    p_group = 1.0 / jnp.sum(jnp.exp(glog - gmax), axis=-1, keepdims=True)
    ei = lax.broadcasted_iota(jnp.int32, (tm, N_EXPERTS), 1)
    in_group = (ei >= gsel * EXPERTS_PER_GROUP) & (ei < (gsel + 1) * EXPERTS_PER_GROUP)
    neg = jnp.float32(-jnp.inf)
    cand = jnp.where(in_group, elog, neg)
    m1, i1 = _first_argmax(cand, ei, N_EXPERTS)
    cand2 = jnp.where(ei == i1, neg, cand)
    m2, i2 = _first_argmax(cand2, ei, N_EXPERTS)
    e2 = jnp.exp(m2 - m1)
    w1 = p_group / (1.0 + e2)
    w2 = p_group * e2 / (1.0 + e2)
    oh1 = (ei == i1).astype(F32)
    oh2 = (ei == i2).astype(F32)
    oh = oh1 + oh2
    ri = lax.broadcasted_iota(jnp.int32, (tm, tm), 0)
    ci = lax.broadcasted_iota(jnp.int32, (tm, tm), 1)
    before = _dot((ri > ci).astype(BF16), oh.astype(BF16)) + carry_sc[...]
    r1 = jnp.sum(before * oh1, axis=-1, keepdims=True)
    r2 = jnp.sum(before * oh2, axis=-1, keepdims=True)
    carry_sc[...] += jnp.sum(oh, axis=0, keepdims=True)
    cnt_ref[...] = carry_sc[...]
    li = lax.broadcasted_iota(jnp.int32, (tm, ROUTE_COLS), 1)
    rec = jnp.zeros((tm, ROUTE_COLS), F32)
    for k, col in enumerate((i1.astype(F32), i2.astype(F32), r1, r2, w1, w2)):
        rec = jnp.where(li == k, col, rec)
    route_ref[...] = rec


def _outproj_router(y_conv, y_delta, w_out, layer, h, norm_g, w_group, b_group, w_router, b_router):
    r, d = h.shape
    kw = y_conv.shape[1]
    tm = _divisor_tile(r, ROUTER_TM, 16)
    w = jnp.concatenate([w_group, w_router], axis=1)
    b = jnp.concatenate([b_group, b_router]).reshape(1, -1)
    nl = w.shape[1]
    assert nl <= ROUTE_LO_LANE
    gap = jnp.zeros((d, ROUTE_LO_LANE - nl), w.dtype)
    w = jnp.concatenate([w, gap, w, gap], axis=1)
    stage_rows = _divisor_tile(2 * kw, WOUT_STAGE_ROWS, SUBLANES)
    kern = functools.partial(_outproj_router_kernel, layer=layer)
    rowblk = lambda width: pl.BlockSpec((tm, width), lambda i: (i, 0))
    const = lambda shape: pl.BlockSpec(shape, lambda i: (0, 0))
    return pl.pallas_call(
        kern,
        grid=(r // tm,),
        in_specs=[rowblk(kw), rowblk(kw), rowblk(d), pl.BlockSpec(memory_space=pl.ANY),
                  const((1, d)), const((d, 2 * ROUTE_LO_LANE)), const((1, nl))],
        out_specs=[rowblk(d), rowblk(d // 2), rowblk(ROUTE_COLS), const((1, N_EXPERTS))],
        out_shape=[
            jax.ShapeDtypeStruct((r, d), F32),
            jax.ShapeDtypeStruct((r, d // 2), jnp.uint32),
            jax.ShapeDtypeStruct((r, ROUTE_COLS), F32),
            jax.ShapeDtypeStruct((1, N_EXPERTS), F32),
        ],
        scratch_shapes=[
            pltpu.VMEM((1, N_EXPERTS), F32),
            pltpu.VMEM((2 * kw, d), BF16),
            pltpu.VMEM((2, stage_rows, d), F32),
            pltpu.SemaphoreType.DMA((2,)),
        ],
        compiler_params=_params(("arbitrary",)),
        name="outproj_router",
    )(y_conv, y_delta, h, w_out, norm_g.reshape(1, d), w, b)


def _dispatch_plan(route, cnt, tm, n_tiles):
    cnt = cnt[0].astype(jnp.int32)
    padded = ((cnt + tm - 1) // tm) * tm
    ends = jnp.cumsum(padded)
    off = ends - padded
    onehot = (route[:, 0:TOP_K, None] == jnp.arange(N_EXPERTS, dtype=F32)).astype(F32)
    pos = jnp.einsum("rke,e->rk", onehot, off.astype(F32), precision=HI) + route[:, TOP_K:2 * TOP_K]
    pos = pos.astype(jnp.int32)
    pos0, pos1 = pos[:, 0], pos[:, 1]
    n_used = ends[-1] // tm
    tiles = jnp.arange(n_tiles, dtype=jnp.int32)
    tile_e = jnp.sum((tiles[:, None] * tm >= ends[None, :]).astype(jnp.int32), axis=1)
    tile_e = jnp.minimum(jnp.where(tiles < n_used, tile_e, tile_e[n_used - 1]), N_EXPERTS - 1)
    experts = jnp.arange(N_EXPERTS, dtype=jnp.int32)
    has_rows = cnt > 0
    slot_tab = (jnp.cumsum(has_rows.astype(jnp.int32)) - 1) % EXPERT_WEIGHT_BUFS
    later = jnp.where(has_rows[None, :] & (experts[None, :] > experts[:, None]), experts[None, :], N_EXPERTS)
    next_tab = jnp.min(later, axis=1)
    next_tab = jnp.where(next_tab == N_EXPERTS, -1, next_tab)
    next2_tab = jnp.where(next_tab >= 0, next_tab[jnp.maximum(next_tab, 0)], -1)
    sel = (tile_e[:, None] == experts[None, :]).astype(jnp.int32)
    per_tile = lambda tab: jnp.sum(sel * tab[None, :], axis=1)
    plan = dict(tile_e=tile_e, slot=per_tile(slot_tab), next_e=per_tile(next_tab), next2_e=per_tile(next2_tab),
                n_used=n_used.reshape(1), pad_start=off + cnt, pad_end=ends)
    return pos0, pos1, plan


def _dispatch_kernel(pos0_ref, pos1_ref, ps_ref, pe_ref, nu_ref, u_ref, xs_ref, zbuf, sem, zsem, *, t_rows, tm, n_tiles,
                     n_steps):
    i = pl.program_id(0)

    def zero_copy(start, n):
        return pltpu.make_async_copy(zbuf.at[pl.ds(0, n)], xs_ref.at[pl.ds(start, n)], zsem)

    def pad_pieces(e):
        start, end = ps_ref[e], pe_ref[e]
        aligned = jnp.minimum((start + (SUBLANES - 1)) & (-SUBLANES), end)
        pieces = [(start + k < aligned, start + k, 1) for k in range(SUBLANES - 1)]
        rem = end - aligned
        at = aligned
        n = tm // 2
        while n >= SUBLANES:
            pieces.append(((rem & n) != 0, pl.multiple_of(at, SUBLANES), n))
            at = at + (rem & n)
            n //= 2
        return pieces

    def for_each_piece(action):
        def pad_body(e, c):
            for take, start, n in pad_pieces(e):
                pl.when(take)(functools.partial(action, start, n))
            return c

        def tail_body(j, c):
            pl.when(j >= nu_ref[0])(functools.partial(action, pl.multiple_of(j * tm, SUBLANES), tm))
            return c

        lax.fori_loop(0, N_EXPERTS, pad_body, 0)
        lax.fori_loop(0, n_tiles, tail_body, 0)

    @pl.when(i == 0)
    def _():
        zbuf[...] = jnp.zeros(zbuf.shape, zbuf.dtype)
        for_each_piece(lambda start, n: zero_copy(start, n).start())

    base = i * t_rows

    def body(r, c):
        src = u_ref.at[pl.ds(r, 1)]
        pltpu.make_async_copy(src, xs_ref.at[pl.ds(pos0_ref[base + r], 1)], sem).start()
        pltpu.make_async_copy(src, xs_ref.at[pl.ds(pos1_ref[base + r], 1)], sem).start()
        return c

    lax.fori_loop(0, t_rows, body, 0, unroll=DMA_UNROLL)
    for _ in range(2):
        pltpu.make_async_copy(u_ref, xs_ref.at[pl.ds(0, t_rows)], sem).wait()

    @pl.when(i == n_steps - 1)
    def _():
        for_each_piece(lambda start, n: zero_copy(start, n).wait())


def _dispatch(u, pos0, pos1, plan, n_slots, tm):
    r, d = u.shape
    t_rows = _divisor_tile(r, DISPATCH_T, SUBLANES)
    n_tiles = n_slots // tm
    assert tm & (tm - 1) == 0 and tm >= 2 * SUBLANES
    kern = functools.partial(_dispatch_kernel, t_rows=t_rows, tm=tm, n_tiles=n_tiles, n_steps=r // t_rows)
    return pl.pallas_call(
        kern,
        grid_spec=pltpu.PrefetchScalarGridSpec(
            num_scalar_prefetch=5,
            grid=(r // t_rows,),
            in_specs=[pl.BlockSpec((t_rows, d), lambda i, *_: (i, 0))],
            out_specs=pl.BlockSpec(memory_space=pl.ANY),
            scratch_shapes=[pltpu.VMEM((tm, d), u.dtype), pltpu.SemaphoreType.DMA(()), pltpu.SemaphoreType.DMA(())],
        ),
        out_shape=jax.ShapeDtypeStruct((n_slots, d), u.dtype),
        compiler_params=_params(("arbitrary",)),
        name="moe_dispatch",
    )(pos0, pos1, plan["pad_start"], plan["pad_end"], plan["n_used"], u)


def _expert_kernel(te_ref, slot_ref, nxt_ref, nxt2_ref, nu_ref, x_ref, wg_hbm, wu_hbm, wd_hbm, y_ref,
                   wg_f, wu_f, wd_f, wg_b, wu_b, wd_b, sems, *, layer):
    i = pl.program_id(0)
    e = te_ref[i]
    slot = slot_ref[i]
    used = i < nu_ref[0]
    first_of_expert = used & ((i == 0) | (te_ref[jnp.maximum(i - 1, 0)] != e))

    def weight_copies(expert, s):
        return [pltpu.make_async_copy(src.at[layer, expert], dst.at[s], sems.at[s])
                for src, dst in ((wg_hbm, wg_f), (wu_hbm, wu_f), (wd_hbm, wd_f))]

    def ring_slot(s, ahead):
        s = s + ahead
        return jnp.where(s >= EXPERT_WEIGHT_BUFS, s - EXPERT_WEIGHT_BUFS, s)

    def start_fetch(expert, s):
        @pl.when(expert >= 0)
        def _():
            for c in weight_copies(expert, s):
                c.start()

    @pl.when(i == 0)
    def _():
        start_fetch(e, slot)
        start_fetch(nxt_ref[i], ring_slot(slot, 1))

    @pl.when(first_of_expert)
    def _():
        for c in weight_copies(e, slot):
            c.wait()
        start_fetch(nxt2_ref[i], ring_slot(slot, 2))

        wg_b[...] = wg_f[slot].astype(BF16)
        wu_b[...] = wu_f[slot].astype(BF16)
        wd_b[...] = wd_f[slot].astype(BF16)

    @pl.when(used)
    def _():
        xa, xb = _unpack_bf16_pairs(x_ref[...])
        half = xa.shape[1]
        hg = _dot(xa, wg_b[0:half, :]) + _dot(xb, wg_b[half:2 * half, :])
        hu = _dot(xa, wu_b[0:half, :]) + _dot(xb, wu_b[half:2 * half, :])
        hid = (_silu(hg) * hu).astype(BF16)
        y_ref[...] = _pack_bf16_pairs(_dot(hid, wd_b[...]).astype(BF16))

    @pl.when(jnp.logical_not(used))
    def _():
        y_ref[...] = jnp.zeros(y_ref.shape, y_ref.dtype)


def _experts(xs, plan, w_gate, w_up, w_down, layer, tm):
    n_slots = xs.shape[0]
    _, _, d, f = w_gate.shape
    assert xs.shape[1] * 2 == d
    n_tiles = n_slots // tm
    used_row = lambda i, te, sl, nx, nx2, nu: (jnp.minimum(i, nu[0] - 1), 0)
    hbm = pl.BlockSpec(memory_space=pl.ANY)
    nbuf = EXPERT_WEIGHT_BUFS
    kern = functools.partial(_expert_kernel, layer=layer)
    return pl.pallas_call(
        kern,
        grid_spec=pltpu.PrefetchScalarGridSpec(
            num_scalar_prefetch=5,
            grid=(n_tiles,),
            in_specs=[pl.BlockSpec((tm, d // 2), used_row), hbm, hbm, hbm],
            out_specs=pl.BlockSpec((tm, d // 2), lambda i, *_: (i, 0)),
            scratch_shapes=[
                pltpu.VMEM((nbuf, d, f), F32), pltpu.VMEM((nbuf, d, f), F32), pltpu.VMEM((nbuf, f, d), F32),
                pltpu.VMEM((d, f), BF16), pltpu.VMEM((d, f), BF16), pltpu.VMEM((f, d), BF16),
                pltpu.SemaphoreType.DMA((nbuf,)),
            ],
        ),
        out_shape=jax.ShapeDtypeStruct((n_slots, d // 2), jnp.uint32),
        compiler_params=_params(("arbitrary",)),
        name="moe_experts",
    )(plan["tile_e"], plan["slot"], plan["next_e"], plan["next2_e"], plan["n_used"], xs, w_gate, w_up, w_down)


def _combine_kernel(pos0_ref, pos1_ref, h_ref, route_ref, ys_ref, o_ref, buf0, buf1, sems, *, t_rows, n_steps):
    i = pl.program_id(0)

    def gather_tile(step, slot):
        base = step * t_rows

        def body(r, c):
            pltpu.make_async_copy(ys_ref.at[pl.ds(pos0_ref[base + r], 1)], buf0.at[slot, pl.ds(r, 1)], sems.at[slot]).start()
            pltpu.make_async_copy(ys_ref.at[pl.ds(pos1_ref[base + r], 1)], buf1.at[slot, pl.ds(r, 1)], sems.at[slot]).start()
            return c

        lax.fori_loop(0, t_rows, body, 0, unroll=DMA_UNROLL)

    @pl.when(i == 0)
    def _():
        gather_tile(0, 0)

    @pl.when(i + 1 < n_steps)
    def _():
        gather_tile(i + 1, (i + 1) % 2)

    slot = i % 2
    for buf in (buf0, buf1):
        pltpu.make_async_copy(ys_ref.at[pl.ds(0, t_rows)], buf.at[slot], sems.at[slot]).wait()
    g = route_ref[...]
    lo0, hi0 = _unpack_bf16_pairs(buf0[slot], F32)
    lo1, hi1 = _unpack_bf16_pairs(buf1[slot], F32)
    half = lo0.shape[1]
    o_ref[:, 0:half] = h_ref[:, 0:half] + g[:, 4:5] * lo0 + g[:, 5:6] * lo1
    o_ref[:, half:2 * half] = h_ref[:, half:2 * half] + g[:, 4:5] * hi0 + g[:, 5:6] * hi1


def _combine(h, route, ys, pos0, pos1):
    r, d = h.shape
    t_rows = _divisor_tile(r, COMBINE_T, SUBLANES)
    n_steps = r // t_rows
    kern = functools.partial(_combine_kernel, t_rows=t_rows, n_steps=n_steps)
    return pl.pallas_call(
        kern,
        grid_spec=pltpu.PrefetchScalarGridSpec(
            num_scalar_prefetch=2,
            grid=(n_steps,),
            in_specs=[
                pl.BlockSpec((t_rows, d), lambda i, p0, p1: (i, 0)),
                pl.BlockSpec((t_rows, ROUTE_COLS), lambda i, p0, p1: (i, 0)),
                pl.BlockSpec(memory_space=pl.ANY),
            ],
            out_specs=pl.BlockSpec((t_rows, d), lambda i, p0, p1: (i, 0)),
            scratch_shapes=[pltpu.VMEM((2, t_rows, d // 2), ys.dtype), pltpu.VMEM((2, t_rows, d // 2), ys.dtype),
                            pltpu.SemaphoreType.DMA((2,))],
        ),
        out_shape=jax.ShapeDtypeStruct((r, d), F32),
        compiler_params=_params(("arbitrary",)),
        name="moe_combine",
    )(pos0, pos1, h, route, ys)


def _routed_experts(h, u, route, cnt, w_gate, w_up, w_down, layer):
    r = h.shape[0]
    tm = EXPERT_TM
    n_tiles = -(-(TOP_K * r + N_EXPERTS * (tm - 1)) // tm)
    pos0, pos1, plan = _dispatch_plan(route, cnt, tm, n_tiles)
    xs = _dispatch(u, pos0, pos1, plan, n_tiles * tm, tm)
    ys = _experts(xs, plan, w_gate, w_up, w_down, layer, tm)
    return _combine(h, route, ys, pos0, pos1)


def _final_kernel(h_ref, g_ref, o_ref):
    x = h_ref[...]
    ms = jnp.mean(x * x, axis=-1, keepdims=True)
    o_ref[0] = x * lax.rsqrt(ms + EPS) * g_ref[...]


def _final_norm(h, norm_g, nb, lp, seq, skip):
    d = h.shape[1]
    t_rows = _divisor_tile(seq, FINAL_T, SUBLANES)
    assert skip % SUBLANES == 0 and lp % SUBLANES == 0
    first_row = lambda b, t: (pl.multiple_of(b * lp + skip + t * t_rows, SUBLANES), 0)
    return pl.pallas_call(
        _final_kernel,
        grid=(nb, seq // t_rows),
        in_specs=[
            pl.BlockSpec((pl.Element(t_rows), pl.Element(d)), first_row),
            pl.BlockSpec((1, d), lambda b, t: (0, 0)),
        ],
        out_specs=pl.BlockSpec((1, t_rows, d), lambda b, t: (b, t, 0)),
        out_shape=jax.ShapeDtypeStruct((nb, seq, d), F32),
        compiler_params=_params(("arbitrary", "arbitrary")),
        name="final_norm",
    )(h, norm_g.reshape(1, d))


def kernel(x, meta, attn_norm, w_in, conv_dw_w, conv_dw_b, conv_ln_g, conv_ln_b, short_conv_w, a_log, dt_bias,
           delta_norm_g, w_out, ffn_norm, w_group, b_group, w_router, b_router, w_gate, w_up, w_down, final_norm):
    nb, seq, d = x.shape
    depth = w_in.shape[0]
    conv_w = conv_dw_w.shape[2]
    delta_w = short_conv_w.shape[2] // 3
    nh = delta_w // HEAD_DIM
    n_main = 2 * conv_w + 4 * delta_w
    assert w_in.shape[2] == n_main + 2 * nh and conv_w == delta_w
    ln = N_META + seq
    pad = (-ln) % CHUNK
    lp = ln + pad
    skip = pad + N_META

    meta_b = jnp.broadcast_to(meta[None].astype(x.dtype), (nb, N_META, d))
    h = jnp.concatenate([jnp.zeros((nb, pad, d), x.dtype), meta_b, x], axis=1).reshape(nb * lp, d)

    w_in_t = jnp.swapaxes(w_in, 1, 2)
    for l in range(depth):
        p, bd_col, bd_row = _inproj(h, attn_norm[l], w_in_t, l, n_main, lp, pad, nb)
        bd_row3 = bd_row.reshape(2 * nh, nb * lp // CHUNK, CHUNK).transpose(1, 0, 2)
        y_conv = _conformer_conv(p, conv_dw_w[l], conv_dw_b[l], conv_ln_g[l], conv_ln_b[l], nb, lp, conv_w)
        y_delta = _gated_deltanet(p, bd_col, bd_row3, short_conv_w[l], a_log[l], dt_bias[l], delta_norm_g[l],
                                  nb, lp, pad, delta_w, 2 * conv_w)
        h, u, route, cnt = _outproj_router(y_conv, y_delta, w_out, l, h, ffn_norm[l], w_group[l], b_group[l],
                                           w_router[l], b_router[l])
        h = _routed_experts(h, u, route, cnt, w_gate, w_up, w_down, l)
    return _final_norm(h, final_norm, nb, lp, seq, skip)
```
